```python
import math
import jax
import jax.numpy as jnp
from jax import lax
import numpy as np

D_MODEL = 4096
BATCH = 4
SEQ = 2048
DEPTH = 2
DEC_BATCH = 8
DEC_SEQ = 1
PAST_LEN = 16384
PAGE_SIZE = 128

HEAD_DIM = 128
A_HEADS = D_MODEL // (2 * HEAD_DIM)
B_HEADS = D_MODEL // (2 * HEAD_DIM)
B_QK_DIM = HEAD_DIM // 2
C_HEADS = D_MODEL // HEAD_DIM
A_WIDTH = A_HEADS * HEAD_DIM
B_WIDTH = B_HEADS * HEAD_DIM
C_WIDTH = C_HEADS * HEAD_DIM
AB_IN = 3 * A_WIDTH + 3 * B_WIDTH
C_IN = 3 * C_WIDTH + C_HEADS
D_FF = ((8 * D_MODEL // 3 + 255) // 256) * 256
PLE_DIM = 256
MOBA_BLOCK = 256
MOBA_TOPK = 3
MOBA_Q_CHUNK = 8
Q_BLOCK = 128
ROPE_THETA = 10000.0
NORM_EPS = 1e-6
SUBLN_EPS = 1e-5
N_EVEN = (DEPTH + 1) // 2
N_ODD = DEPTH // 2

kernel_name = 'moba_diff_fox_macaron_step'


def rmsnorm(x, g, eps=NORM_EPS):
    xf = x.astype(jnp.float32)
    y = xf * lax.rsqrt(jnp.mean(xf * xf, axis=-1, keepdims=True) + eps)
    return (y * g.astype(jnp.float32)).astype(x.dtype)


def swiglu(x, w_gate, w_up, w_down):
    return (jax.nn.silu(x @ w_gate) * (x @ w_up)) @ w_down


def rope(x, pos):
    half = x.shape[-1] // 2
    inv_freq = jnp.power(ROPE_THETA, -jnp.arange(half, dtype=jnp.float32) / half)
    ang = pos.astype(jnp.float32)[:, None] * inv_freq[None, :]
    cos = jnp.cos(ang)[:, None, :]
    sin = jnp.sin(ang)[:, None, :]
    xf = x.astype(jnp.float32)
    x1, x2 = xf[..., :half], xf[..., half:]
    return jnp.concatenate([x1 * cos - x2 * sin, x2 * cos + x1 * sin], axis=-1).astype(x.dtype)


def rope_halves(x, pos):
    b, t, h, d = x.shape
    return rope(x.reshape(b, t, 2 * h, d // 2), pos).reshape(b, t, h, d)


def gather_pages(pool, page_table):
    g = pool[page_table]
    return g.reshape((g.shape[0], g.shape[1] * g.shape[2]) + g.shape[3:])


def split_heads(z, n_heads):
    return z.reshape(z.shape[0], z.shape[1], n_heads, HEAD_DIM)


def query_blocks(z, size):
    b, t = z.shape[:2]
    return jnp.moveaxis(z.reshape((b, t // size, size) + z.shape[2:]), 1, 0)


def unblock(z):
    z = jnp.moveaxis(z, 0, 1)
    return z.reshape((z.shape[0], z.shape[1] * z.shape[2]) + z.shape[3:])


def moba_blocks(k, v):
    b, l, h, d = k.shape
    nb = -(-l // MOBA_BLOCK)
    pad = ((0, 0), (0, nb * MOBA_BLOCK - l), (0, 0), (0, 0))
    kb = jnp.pad(k, pad).reshape(b, nb, MOBA_BLOCK, h, d)
    vb = jnp.pad(v, pad).reshape(b, nb, MOBA_BLOCK, h, d)
    k_mean = jnp.mean(kb, axis=2, dtype=jnp.float32)
    return kb, vb, k_mean


def moba_attend(q, q_pos, kb, vb, k_mean):
    bt, nq, h, _ = q.shape
    nb = kb.shape[1]
    tk = min(MOBA_TOPK, nb)
    own = q_pos // MOBA_BLOCK
    gate = jnp.einsum('bqhd,bnhd->bhqn', q.astype(jnp.float32), k_mean)
    gate = jnp.where(jnp.arange(nb)[None, :] < own[:, None], gate, -jnp.inf)
    _, sel = lax.top_k(gate, tk)
    own_idx = jnp.broadcast_to(own[None, None, :, None], (bt, h, nq, 1)).astype(sel.dtype)
    idx = jnp.concatenate([own_idx, sel], axis=-1)
    slot_ok = jnp.concatenate([jnp.ones((nq, 1), bool), jnp.arange(tk)[None, :] < own[:, None]], axis=-1)
    bi = jnp.arange(bt)[:, None, None, None]
    hi = jnp.arange(h)[None, :, None, None]
    k_sel = kb[bi, idx, :, hi]
    v_sel = vb[bi, idx, :, hi]
    key_pos = idx[..., None] * MOBA_BLOCK + jnp.arange(MOBA_BLOCK)
    mask = (key_pos <= q_pos[:, None, None]) & slot_ok[:, :, None]
    s = jnp.einsum('bqhd,bhqnkd->bhqnk', q, k_sel).astype(jnp.float32) * (HEAD_DIM ** -0.5)
    s = jnp.where(mask, s, -jnp.inf).reshape(bt, h, nq, -1)
    p = jax.nn.softmax(s, axis=-1).reshape(mask.shape).astype(v_sel.dtype)
    return jnp.einsum('bhqnk,bhqnkd->bqhd', p, v_sel)


def moba_prompt(q, k, v, pos):
    kb, vb, k_mean = moba_blocks(k, v)
    out = lax.map(lambda a: moba_attend(a[0], a[1], kb, vb, k_mean),
                  (query_blocks(q, MOBA_Q_CHUNK), pos.reshape(-1, MOBA_Q_CHUNK)))
    return unblock(out)


def diff_weights(s, mask, lam):
    p = jax.nn.softmax(jnp.where(mask, s, -jnp.inf), axis=-1)
    return p[0] - lam * p[1]


def diff_prompt(q, k, v, pos, lam):
    b, t, h, d = q.shape
    k2 = k.reshape(b, t, h, 2, B_QK_DIM)
    scale = B_QK_DIM ** -0.5

    def block(args):
        qb, qp = args
        s = jnp.einsum('bqhmd,bkhmd->mbhqk', qb, k2).astype(jnp.float32) * scale
        w = diff_weights(s, pos[None, :] <= qp[:, None], lam)
        return jnp.einsum('bhqk,bkhd->bqhd', w.astype(v.dtype), v)

    qb = query_blocks(q.reshape(b, t, h, 2, B_QK_DIM), Q_BLOCK)
    return unblock(lax.map(block, (qb, pos.reshape(-1, Q_BLOCK))))


def diff_sample(q, k_past, v_past, k_new, v_new, pos, lam):
    b, nq, h, d = q.shape
    n_past = k_past.shape[1]
    q2 = q.reshape(b, nq, h, 2, B_QK_DIM)
    s = jnp.concatenate([
        jnp.einsum('bqhmd,bkhmd->mbhqk', q2, k_past.reshape(b, n_past, h, 2, B_QK_DIM)),
        jnp.einsum('bqhmd,bkhmd->mbhqk', q2, k_new.reshape(b, nq, h, 2, B_QK_DIM)),
    ], axis=-1).astype(jnp.float32) * (B_QK_DIM ** -0.5)
    mask = jnp.concatenate([jnp.ones((nq, n_past), bool), pos[None, :] <= pos[:, None]], axis=-1)
    w = diff_weights(s, mask, lam).astype(v_past.dtype)
    return (jnp.einsum('bhqk,bkhd->bqhd', w[..., :n_past], v_past)
            + jnp.einsum('bhqk,bkhd->bqhd', w[..., n_past:], v_new))


def diff_finish(o, subln, lam_init):
    return rmsnorm(o, subln, SUBLN_EPS) * (1.0 - lam_init)


def fox_prompt(q, k, v, logf, pos):
    c = jnp.cumsum(logf, axis=1).transpose(0, 2, 1)
    b, h, t = c.shape
    scale = HEAD_DIM ** -0.5

    def block(args):
        qb, qp, cq = args
        s = jnp.einsum('bqhd,bkhd->bhqk', qb, k).astype(jnp.float32) * scale + (cq[..., None] - c[:, :, None, :])
        p = jax.nn.softmax(jnp.where(pos[None, :] <= qp[:, None], s, -jnp.inf), axis=-1)
        return jnp.einsum('bhqk,bkhd->bqhd', p.astype(v.dtype), v)

    cb = jnp.moveaxis(c.reshape(b, h, t // Q_BLOCK, Q_BLOCK), 2, 0)
    return unblock(lax.map(block, (query_blocks(q, Q_BLOCK), pos.reshape(-1, Q_BLOCK), cb)))


def fox_sample(q, k_past, v_past, lf_past, k_new, v_new, lf_new, pos):
    n_past = k_past.shape[1]
    nq = q.shape[1]
    c = jnp.cumsum(jnp.concatenate([lf_past.astype(jnp.float32), lf_new], axis=1), axis=1).transpose(0, 2, 1)
    cq = c[:, :, n_past:]
    s = jnp.concatenate([jnp.einsum('bqhd,bkhd->bhqk', q, k_past),
                         jnp.einsum('bqhd,bkhd->bhqk', q, k_new)], axis=-1).astype(jnp.float32)
    s = s * (HEAD_DIM ** -0.5) + (cq[..., None] - c[:, :, None, :])
    mask = jnp.concatenate([jnp.ones((nq, n_past), bool), pos[None, :] <= pos[:, None]], axis=-1)
    p = jax.nn.softmax(jnp.where(mask, s, -jnp.inf), axis=-1).astype(v_past.dtype)
    return (jnp.einsum('bhqk,bkhd->bqhd', p[..., :n_past], v_past)
            + jnp.einsum('bhqk,bkhd->bqhd', p[..., n_past:], v_new))


def ab_project(h, w_in, pos):
    offs = [A_WIDTH, 2 * A_WIDTH, 3 * A_WIDTH, 3 * A_WIDTH + B_WIDTH, 3 * A_WIDTH + 2 * B_WIDTH]
    qa, ka, va, qb, kb, vb = jnp.split(h @ w_in, offs, axis=-1)
    return (rope(split_heads(qa, A_HEADS), pos), rope(split_heads(ka, A_HEADS), pos), split_heads(va, A_HEADS),
            rope_halves(split_heads(qb, B_HEADS), pos), rope_halves(split_heads(kb, B_HEADS), pos),
            split_heads(vb, B_HEADS))


def c_project(h, w_in, b_f):
    q, k, v, f_logit = jnp.split(h @ w_in, [C_WIDTH, 2 * C_WIDTH, 3 * C_WIDTH], axis=-1)
    logf = jax.nn.log_sigmoid(f_logit.astype(jnp.float32) + b_f.astype(jnp.float32))
    return split_heads(q, C_HEADS), split_heads(k, C_HEADS), split_heads(v, C_HEADS), logf


def merge_heads(outs, w_out):
    o = jnp.concatenate(outs, axis=2)
    return o.reshape(o.shape[0], o.shape[1], -1) @ w_out


def setup_inputs(seed: int = 0) -> dict:
    key = jax.random.key(seed)
    ks = iter(jax.random.split(key, 48))
    f32 = jnp.float32

    def normal(shape, scale):
        return jax.random.normal(next(ks), shape, f32) * scale

    def gain(shape):
        return 1.0 + 0.05 * jax.random.normal(next(ks), shape, f32)

    n_pages = PAST_LEN // PAGE_SIZE
    n_used = DEC_BATCH * n_pages
    n_pool = n_used + n_used // 4
    page_table = jax.random.permutation(next(ks), n_pool)[:n_used].reshape(DEC_BATCH, n_pages).astype(jnp.int32)
    a_pool = (N_EVEN, n_pool, PAGE_SIZE, A_HEADS, HEAD_DIM)
    b_pool = (N_EVEN, n_pool, PAGE_SIZE, B_HEADS, HEAD_DIM)
    c_pool = (N_ODD, n_pool, PAGE_SIZE, C_HEADS, HEAD_DIM)
    return {
        'x_prompt': jax.random.normal(next(ks), (BATCH, SEQ, D_MODEL), f32),
        'x_sample': jax.random.normal(next(ks), (DEC_BATCH, DEC_SEQ, D_MODEL), f32),
        'cache_a_k': jax.random.normal(next(ks), a_pool, f32),
        'cache_a_v': jax.random.normal(next(ks), a_pool, f32),
        'cache_b_k': jax.random.normal(next(ks), b_pool, f32),
        'cache_b_v': jax.random.normal(next(ks), b_pool, f32),
        'cache_c_k': jax.random.normal(next(ks), c_pool, f32),
        'cache_c_v': jax.random.normal(next(ks), c_pool, f32),
        'cache_c_logf': jax.nn.log_sigmoid(jax.random.uniform(next(ks), (N_ODD, n_pool, PAGE_SIZE, C_HEADS), f32, 1.0, 5.0)),
        'page_table': page_table,
        'p_prompt': jax.random.normal(next(ks), (DEPTH, BATCH, SEQ, PLE_DIM), f32),
        'p_sample': jax.random.normal(next(ks), (DEPTH, DEC_BATCH, DEC_SEQ, PLE_DIM), f32),
        'ffn1_norm': gain((DEPTH, D_MODEL)),
        'ffn1_w_gate': normal((DEPTH, D_MODEL, D_FF), D_MODEL ** -0.5),
        'ffn1_w_up': normal((DEPTH, D_MODEL, D_FF), D_MODEL ** -0.5),
        'ffn1_w_down': normal((DEPTH, D_FF, D_MODEL), D_FF ** -0.5),
        'mix_norm': gain((DEPTH, D_MODEL)),
        'ab_w_in': normal((N_EVEN, D_MODEL, AB_IN), D_MODEL ** -0.5),
        'b_lambda_q1': normal((N_EVEN, B_QK_DIM), 0.1),
        'b_lambda_k1': normal((N_EVEN, B_QK_DIM), 0.1),
        'b_lambda_q2': normal((N_EVEN, B_QK_DIM), 0.1),
        'b_lambda_k2': normal((N_EVEN, B_QK_DIM), 0.1),
        'b_subln': gain((N_EVEN, HEAD_DIM)),
        'ab_w_out': normal((N_EVEN, A_WIDTH + B_WIDTH, D_MODEL), (A_WIDTH + B_WIDTH) ** -0.5),
        'c_w_in': normal((N_ODD, D_MODEL, C_IN), D_MODEL ** -0.5),
        'c_b_f': jax.random.uniform(next(ks), (N_ODD, C_HEADS), f32, 1.0, 5.0),
        'c_w_out': normal((N_ODD, C_WIDTH, D_MODEL), C_WIDTH ** -0.5),
        'ffn2_norm': gain((DEPTH, D_MODEL)),
        'ffn2_w_gate': normal((DEPTH, D_MODEL, D_FF), D_MODEL ** -0.5),
        'ffn2_w_up': normal((DEPTH, D_MODEL, D_FF), D_MODEL ** -0.5),
        'ffn2_w_down': normal((DEPTH, D_FF, D_MODEL), D_FF ** -0.5),
        'ple_norm': gain((DEPTH, D_MODEL)),
        'ple_w_gate': normal((DEPTH, D_MODEL, D_MODEL), D_MODEL ** -0.5),
        'ple_w_proj': normal((DEPTH, PLE_DIM, D_MODEL), PLE_DIM ** -0.5),
        'final_norm': gain((D_MODEL,)),
    }


def reference(x_prompt, x_sample, cache_a_k, cache_a_v, cache_b_k, cache_b_v, cache_c_k, cache_c_v,
              cache_c_logf, page_table, p_prompt, p_sample, ffn1_norm, ffn1_w_gate, ffn1_w_up, ffn1_w_down,
              mix_norm, ab_w_in, b_lambda_q1, b_lambda_k1, b_lambda_q2, b_lambda_k2, b_subln, ab_w_out,
              c_w_in, c_b_f, c_w_out, ffn2_norm, ffn2_w_gate, ffn2_w_up, ffn2_w_down, ple_norm, ple_w_gate,
              ple_w_proj, final_norm):
    seq = x_prompt.shape[1]
    dec_seq = x_sample.shape[1]
    past_len = page_table.shape[1] * cache_a_k.shape[2]
    pos_p = jnp.arange(seq, dtype=jnp.int32)
    pos_s = past_len + jnp.arange(dec_seq, dtype=jnp.int32)
    xp, xs = x_prompt, x_sample
    akp, avp, aks, avs = [], [], [], []
    bkp, bvp, bks, bvs = [], [], [], []
    ckp, cvp, cfp, cks, cvs, cfs = [], [], [], [], [], []
    for i in range(DEPTH):
        xp = xp + 0.5 * swiglu(rmsnorm(xp, ffn1_norm[i]), ffn1_w_gate[i], ffn1_w_up[i], ffn1_w_down[i])
        xs = xs + 0.5 * swiglu(rmsnorm(xs, ffn1_norm[i]), ffn1_w_gate[i], ffn1_w_up[i], ffn1_w_down[i])
        hp = rmsnorm(xp, mix_norm[i])
        hs = rmsnorm(xs, mix_norm[i])
        if i % 2 == 0:
            e = i // 2
            lam_init = 0.8 - 0.6 * math.exp(-0.3 * i)
            lam = (jnp.exp(jnp.sum(b_lambda_q1[e].astype(jnp.float32) * b_lambda_k1[e].astype(jnp.float32)))
                   - jnp.exp(jnp.sum(b_lambda_q2[e].astype(jnp.float32) * b_lambda_k2[e].astype(jnp.float32)))
                   + lam_init)
            qa, ka, va, qb, kb, vb = ab_project(hp, ab_w_in[e], pos_p)
            a_out = moba_prompt(qa, ka, va, pos_p)
            b_out = diff_finish(diff_prompt(qb, kb, vb, pos_p, lam), b_subln[e], lam_init)
            yp = merge_heads([a_out, b_out], ab_w_out[e])
            akp.append(ka); avp.append(va); bkp.append(kb); bvp.append(vb)
            qa, ka, va, qb, kb, vb = ab_project(hs, ab_w_in[e], pos_s)
            ka_all = jnp.concatenate([gather_pages(cache_a_k[e], page_table), ka], axis=1)
            va_all = jnp.concatenate([gather_pages(cache_a_v[e], page_table), va], axis=1)
            kbl, vbl, k_mean = moba_blocks(ka_all, va_all)
            a_out = moba_attend(qa, pos_s, kbl, vbl, k_mean)
            b_out = diff_sample(qb, gather_pages(cache_b_k[e], page_table), gather_pages(cache_b_v[e], page_table),
                                kb, vb, pos_s, lam)
            b_out = diff_finish(b_out, b_subln[e], lam_init)
            ys = merge_heads([a_out, b_out], ab_w_out[e])
            aks.append(ka); avs.append(va); bks.append(kb); bvs.append(vb)
        else:
            o = i // 2
            q, k, v, logf = c_project(hp, c_w_in[o], c_b_f[o])
            yp = merge_heads([fox_prompt(q, k, v, logf, pos_p)], c_w_out[o])
            ckp.append(k); cvp.append(v); cfp.append(logf.astype(cache_c_logf.dtype))
            q, k, v, logf = c_project(hs, c_w_in[o], c_b_f[o])
            c_out = fox_sample(q, gather_pages(cache_c_k[o], page_table), gather_pages(cache_c_v[o], page_table),
                               gather_pages(cache_c_logf[o], page_table), k, v, logf, pos_s)
            ys = merge_heads([c_out], c_w_out[o])
            cks.append(k); cvs.append(v); cfs.append(logf.astype(cache_c_logf.dtype))
        xp = xp + yp
        xs = xs + ys
        xp = xp + 0.5 * swiglu(rmsnorm(xp, ffn2_norm[i]), ffn2_w_gate[i], ffn2_w_up[i], ffn2_w_down[i])
        xs = xs + 0.5 * swiglu(rmsnorm(xs, ffn2_norm[i]), ffn2_w_gate[i], ffn2_w_up[i], ffn2_w_down[i])
        xp = xp + jax.nn.sigmoid(rmsnorm(xp, ple_norm[i]) @ ple_w_gate[i]) * (p_prompt[i] @ ple_w_proj[i])
        xs = xs + jax.nn.sigmoid(rmsnorm(xs, ple_norm[i]) @ ple_w_gate[i]) * (p_sample[i] @ ple_w_proj[i])
    y_prompt = rmsnorm(xp, final_norm)
    y_sample = rmsnorm(xs, final_norm)
    return (y_prompt, y_sample,
            jnp.stack(akp), jnp.stack(avp), jnp.stack(aks), jnp.stack(avs),
            jnp.stack(bkp), jnp.stack(bvp), jnp.stack(bks), jnp.stack(bvs),
            jnp.stack(ckp), jnp.stack(cvp), jnp.stack(cfp), jnp.stack(cks), jnp.stack(cvs), jnp.stack(cfs))
```

```python
import functools
import math

import jax
import jax.numpy as jnp
from jax import lax
from jax.experimental import pallas as pl
from jax.experimental.pallas import tpu as pltpu

F32 = jnp.float32
BF16 = jnp.bfloat16

HEAD_DIM = 128
MOBA_BLOCK = 256
MOBA_TOPK = 3
ROPE_THETA = 10000.0
NORM_EPS = 1e-6
SUBLN_EPS = 1e-5
NEG = -1e30

V7X_VMEM_BYTES = 64 * 1024 * 1024
VMEM_LIMIT = V7X_VMEM_BYTES - 8 * 1024 * 1024
LANES = 128
SAMPLE_ROWS = 16


def _params(*sem):
    return pltpu.CompilerParams(dimension_semantics=sem, vmem_limit_bytes=VMEM_LIMIT)


def _split3(x):
    a = x.astype(BF16)
    r = x - a.astype(F32)
    b = r.astype(BF16)
    c = (r - b.astype(F32)).astype(BF16)
    return a, b, c


def _dot(a, b):
    return jnp.dot(a, b, preferred_element_type=F32)


def _dot_nt(a, b):
    return lax.dot_general(a, b, (((1,), (1,)), ((), ())), preferred_element_type=F32)


def _cast_kernel(w_ref, o_ref):
    o_ref[...] = w_ref[...].astype(BF16)


def _cast_bf16(w):
    l, k, n = w.shape
    rows = 128
    return pl.pallas_call(
        _cast_kernel,
        grid=(l, k // rows),
        in_specs=[pl.BlockSpec((1, rows, n), lambda i, j: (i, j, 0))],
        out_specs=pl.BlockSpec((1, rows, n), lambda i, j: (i, j, 0)),
        out_shape=jax.ShapeDtypeStruct(w.shape, BF16),
        compiler_params=_params("parallel", "parallel"),
        name="cast_bf16",
    )(w)


def _rms(x, g, eps):
    return x * lax.rsqrt(jnp.mean(x * x, axis=-1, keepdims=True) + eps) * g


def _rmsnorm_kernel(x_ref, g_ref, o_ref):
    o_ref[...] = _rms(x_ref[...], g_ref[...], NORM_EPS).astype(o_ref.dtype)


def _rmsnorm(x, g, out_dtype):
    m, d = x.shape
    bm = min(512, m)
    return pl.pallas_call(
        _rmsnorm_kernel,
        grid=(m // bm,),
        in_specs=[pl.BlockSpec((bm, d), lambda i: (i, 0)),
                  pl.BlockSpec((1, d), lambda i: (0, 0))],
        out_specs=pl.BlockSpec((bm, d), lambda i: (i, 0)),
        out_shape=jax.ShapeDtypeStruct((m, d), out_dtype),
        compiler_params=_params("parallel"),
        name="rmsnorm",
    )(x, g.reshape(1, d))


FFN_SLAB = 256


def _ffn_kernel(x_ref, g_ref, wg_ref, wu_ref, wd_ref, o_ref, xn_ref):
    @pl.when(pl.program_id(1) == 0)
    def _():
        x = x_ref[...]
        xn_ref[...] = _rms(x, g_ref[...], NORM_EPS).astype(BF16)
        o_ref[...] = x

    xn = xn_ref[...]
    gate = _dot(xn, wg_ref[0])
    up = _dot(xn, wu_ref[0])
    act = (0.5 * (gate * jax.nn.sigmoid(gate)) * up).astype(BF16)
    o_ref[...] += _dot(act, wd_ref[0])


def _ffn(x, g, wg, wu, wd, layer):
    m, d = x.shape
    f = wg.shape[2]
    bm = min(512, m)
    return pl.pallas_call(
        _ffn_kernel,
        grid=(m // bm, f // FFN_SLAB),
        in_specs=[pl.BlockSpec((bm, d), lambda i, j: (i, 0), pipeline_mode=pl.Buffered(1)),
                  pl.BlockSpec((1, d), lambda i, j: (0, 0)),
                  pl.BlockSpec((1, d, FFN_SLAB), lambda i, j: (layer, 0, j)),
                  pl.BlockSpec((1, d, FFN_SLAB), lambda i, j: (layer, 0, j)),
                  pl.BlockSpec((1, FFN_SLAB, d), lambda i, j: (layer, j, 0))],
        out_specs=pl.BlockSpec((bm, d), lambda i, j: (i, 0)),
        out_shape=jax.ShapeDtypeStruct((m, d), F32),
        scratch_shapes=[pltpu.VMEM((bm, d), BF16)],
        compiler_params=_params("parallel", "arbitrary"),
        name="ffn",
    )(x, g.reshape(1, d), wg, wu, wd)


def _rope_tables(pos, kind):
    width = HEAD_DIM if kind == "a" else HEAD_DIM // 2
    half = width // 2
    inv_freq = jnp.power(ROPE_THETA, -jnp.arange(half, dtype=F32) / half)
    ang = pos.astype(F32)[:, None] * inv_freq[None, :]
    cos, sin = jnp.cos(ang), jnp.sin(ang)
    reps = HEAD_DIM // width
    cos_t = jnp.tile(jnp.concatenate([cos, cos], axis=-1), (1, reps))
    sin_t = jnp.tile(jnp.concatenate([-sin, sin], axis=-1), (1, reps))
    return cos_t, sin_t


def _rope_head(x, cos, sin, kind):
    if kind == "a":
        partner = pltpu.roll(x, HEAD_DIM // 2, 1)
    else:
        lane = lax.broadcasted_iota(jnp.int32, x.shape, 1)
        first = (lane & (HEAD_DIM // 2 - 1)) < (HEAD_DIM // 4)
        partner = jnp.where(first, pltpu.roll(x, HEAD_DIM - HEAD_DIM // 4, 1), pltpu.roll(x, HEAD_DIM // 4, 1))
    return x * cos + partner * sin


def _proj_kernel(*refs, rope, n_out):
    if rope is None:
        a_ref, w_ref = refs[:2]
        outs = refs[2:]
    else:
        a_ref, w_ref, cos_ref, sin_ref = refs[:4]
        outs = refs[4:]
    acc = _dot(a_ref[...], w_ref[0].astype(BF16))
    if rope is None:
        for o in outs[:n_out]:
            o[...] = acc.astype(o.dtype)
        return
    cos, sin = cos_ref[...], sin_ref[...]
    for h in range(acc.shape[1] // HEAD_DIM):
        cols = slice(h * HEAD_DIM, (h + 1) * HEAD_DIM)
        head = _rope_head(acc[:, cols], cos, sin, rope)
        for o in outs[:n_out]:
            o[:, cols] = head.astype(o.dtype)


def _proj(a, w, layer, col0, width, out_dtypes, rope=None, tables=None):
    m, k = a.shape
    bm = min(1024, m) if rope is None else min(1024, m, tables[0].shape[0])
    bn = min(512, width)
    assert col0 % bn == 0 and width % bn == 0
    in_specs = [pl.BlockSpec((bm, k), lambda i, j: (i, 0)),
                pl.BlockSpec((1, k, bn), lambda i, j: (layer, 0, col0 // bn + j))]
    args = [a, w]
    if rope is not None:
        nt = tables[0].shape[0] // bm
        in_specs += [pl.BlockSpec((bm, HEAD_DIM), lambda i, j: (i % nt, 0))] * 2
        args += list(tables)
    outs = pl.pallas_call(
        functools.partial(_proj_kernel, rope=rope, n_out=len(out_dtypes)),
        grid=(m // bm, width // bn),
        in_specs=in_specs,
        out_specs=[pl.BlockSpec((bm, bn), lambda i, j: (i, j))] * len(out_dtypes),
        out_shape=[jax.ShapeDtypeStruct((m, width), dt) for dt in out_dtypes],
        compiler_params=_params("parallel", "parallel"),
        name="proj_" + (rope or "plain"),
    )(*args)
    return outs


def _logf_kernel(a_ref, w_ref, b_ref, o_ref):
    z = _dot(a_ref[...], w_ref[...].astype(BF16)) + b_ref[...]
    o_ref[...] = jnp.minimum(z, 0.0) - jnp.log1p(jnp.exp(-jnp.abs(z)))


def _logf_proj(a, w_f, b_f):
    m, k = a.shape
    bm = min(1024, m)
    return pl.pallas_call(
        _logf_kernel,
        grid=(m // bm,),
        in_specs=[pl.BlockSpec((bm, k), lambda i: (i, 0)),
                  pl.BlockSpec((k, LANES), lambda i: (0, 0)),
                  pl.BlockSpec((1, LANES), lambda i: (0, 0))],
        out_specs=pl.BlockSpec((bm, LANES), lambda i: (i, 0)),
        out_shape=jax.ShapeDtypeStruct((m, LANES), F32),
        compiler_params=_params("parallel"),
        name="logf_proj",
    )(a, w_f, b_f)


def _out_proj_kernel(*refs, n_lhs):
    a_refs = refs[:n_lhs]
    w_ref, x_ref, o_ref = refs[n_lhs:]
    acc = x_ref[...]
    k0 = 0
    for a_ref in a_refs:
        kk = a_ref.shape[1]
        acc = acc + _dot(a_ref[...], w_ref[0, k0:k0 + kk, :])
        k0 += kk
    o_ref[...] = acc


def _out_proj(lhs, w, layer, x):
    m, n = x.shape
    k = w.shape[1]
    bm = min(1024, m)
    bn = min(512, n)
    in_specs = [pl.BlockSpec((bm, a.shape[1]), lambda i, j: (i, 0)) for a in lhs]
    in_specs += [pl.BlockSpec((1, k, bn), lambda i, j: (layer, 0, j)),
                 pl.BlockSpec((bm, bn), lambda i, j: (i, j))]
    return pl.pallas_call(
        functools.partial(_out_proj_kernel, n_lhs=len(lhs)),
        grid=(m // bm, n // bn),
        in_specs=in_specs,
        out_specs=pl.BlockSpec((bm, bn), lambda i, j: (i, j)),
        out_shape=jax.ShapeDtypeStruct((m, n), F32),
        compiler_params=_params("parallel", "parallel"),
        name="out_proj",
    )(*lhs, w, x)


def _ple_kernel(a_ref, wg_ref, p_ref, wp_ref, x_ref, o_ref):
    gate = jax.nn.sigmoid(_dot(a_ref[...], wg_ref[0]))
    emb = _dot(p_ref[...].astype(BF16), wp_ref[0].astype(BF16))
    o_ref[...] = x_ref[...] + gate * emb


def _ple(a, wg, p, wp, layer, x):
    m, n = x.shape
    k = a.shape[1]
    kp = p.shape[1]
    bm = min(1024, m)
    bn = min(512, n)
    return pl.pallas_call(
        _ple_kernel,
        grid=(m // bm, n // bn),
        in_specs=[pl.BlockSpec((bm, k), lambda i, j: (i, 0)),
                  pl.BlockSpec((1, k, bn), lambda i, j: (layer, 0, j)),
                  pl.BlockSpec((bm, kp), lambda i, j: (i, 0)),
                  pl.BlockSpec((1, kp, bn), lambda i, j: (layer, 0, j)),
                  pl.BlockSpec((bm, bn), lambda i, j: (i, j))],
        out_specs=pl.BlockSpec((bm, bn), lambda i, j: (i, j)),
        out_shape=jax.ShapeDtypeStruct((m, n), F32),
        compiler_params=_params("parallel", "parallel"),
        name="ple",
    )(a, wg, p, wp, x)


ATTN_TQ = 256


def _softmax_rows(s):
    m = jnp.max(s, axis=-1, keepdims=True)
    e = jnp.exp(s - m)
    return e / jnp.sum(e, axis=-1, keepdims=True)


def _causal_mask(i, kend):
    r = lax.broadcasted_iota(jnp.int32, (ATTN_TQ, kend), 0) + i * ATTN_TQ
    c = lax.broadcasted_iota(jnp.int32, (ATTN_TQ, kend), 1)
    return c <= r


def _moba_prompt_kernel(q_ref, k_ref, v_ref, kf_ref, o_ref):
    t = q_ref.shape[0]
    nb = t // MOBA_BLOCK
    assert ATTN_TQ == MOBA_BLOCK and nb <= LANES
    scale = HEAD_DIM ** -0.5
    km = jnp.sum(kf_ref[...].reshape(nb, MOBA_BLOCK, HEAD_DIM), axis=1) * (1.0 / MOBA_BLOCK)
    km = jnp.concatenate([km, jnp.zeros((LANES - nb, HEAD_DIM), F32)], axis=0)
    km_hi = km.astype(BF16)
    km_lo = (km - km_hi.astype(F32)).astype(BF16)
    for i in range(t // ATTN_TQ):
        rows = slice(i * ATTN_TQ, (i + 1) * ATTN_TQ)
        kend = (i + 1) * MOBA_BLOCK
        q = q_ref[rows, :]
        s = _dot_nt(q, k_ref[0:kend, :]) * scale
        mask = _causal_mask(i, kend)
        if i > MOBA_TOPK:
            gate = _dot_nt(q, km_hi) + _dot_nt(q, km_lo)
            lane = lax.broadcasted_iota(jnp.int32, gate.shape, 1)
            blk = lax.broadcasted_iota(jnp.int32, (ATTN_TQ, kend), 1) >> (MOBA_BLOCK.bit_length() - 1)
            mask = mask & (blk == i)
            for n in range(i):
                g_n = gate[:, n:n + 1]
                ahead = ((gate > g_n) | ((gate == g_n) & (lane < n))) & (lane < i)
                rank = jnp.sum(ahead.astype(F32), axis=-1, keepdims=True)
                mask = mask | ((blk == n) & (rank < MOBA_TOPK))
        p = _softmax_rows(jnp.where(mask, s, NEG)).astype(BF16)
        o_ref[rows, :] = _dot(p, v_ref[0:kend, :]).astype(o_ref.dtype)


def _diff_lambda(lq1, lk1, lq2, lk2, lam_init):
    return (jnp.exp(jnp.sum(lq1 * lk1, axis=-1, keepdims=True))
            - jnp.exp(jnp.sum(lq2 * lk2, axis=-1, keepdims=True)) + lam_init)


def _diff_finish(o, subln, lam_init):
    return _rms(o, subln, SUBLN_EPS) * (1.0 - lam_init)


def _diff_prompt_kernel(q_ref, k_ref, v_ref, lq1_ref, lk1_ref, lq2_ref, lk2_ref, g_ref, o_ref, *, lam_init):
    t = q_ref.shape[0]
    half = HEAD_DIM // 2
    scale = half ** -0.5
    lam = _diff_lambda(lq1_ref[...], lk1_ref[...], lq2_ref[...], lk2_ref[...], lam_init)
    q = q_ref[...]
    lane = lax.broadcasted_iota(jnp.int32, q.shape, 1)
    zero = jnp.zeros_like(q)
    q1 = jnp.where(lane < half, q, zero)
    q2 = jnp.where(lane >= half, q, zero)
    for i in range(t // ATTN_TQ):
        rows = slice(i * ATTN_TQ, (i + 1) * ATTN_TQ)
        kend = (i + 1) * ATTN_TQ
        k = k_ref[0:kend, :]
        mask = _causal_mask(i, kend)
        p1 =_softmax_rows(jnp.where(mask, _dot_nt(q1[rows], k) * scale, NEG))
        p2 = _softmax_rows(jnp.where(mask, _dot_nt(q2[rows], k) * scale, NEG))
        w = (p1 - lam * p2).astype(BF16)
        o = _dot(w, v_ref[0:kend, :])
        o_ref[rows, :] = _diff_finish(o, g_ref[...], lam_init).astype(o_ref.dtype)


def _fox_prompt_kernel(q_ref, k_ref, v_ref, crow_ref, ccol_ref, o_ref):
    t = q_ref.shape[0]
    scale = HEAD_DIM ** -0.5
    for i in range(t // ATTN_TQ):
        rows = slice(i * ATTN_TQ, (i + 1) * ATTN_TQ)
        kend = (i + 1) * ATTN_TQ
        s = _dot_nt(q_ref[rows, :], k_ref[0:kend, :]) * scale
        s = s + (ccol_ref[0, 0, rows, :] - crow_ref[0, 0, :, 0:kend])
        p = _softmax_rows(jnp.where(_causal_mask(i, kend), s, NEG)).astype(BF16)
        o_ref[rows, :] = _dot(p, v_ref[0:kend, :]).astype(o_ref.dtype)


def _head_spec(t):
    return pl.BlockSpec((t, HEAD_DIM), lambda b, h: (b, h))


def _vec_spec(n):
    return pl.BlockSpec((1, n), lambda b, h: (0, 0))


def _prompt_attn(kernel, name, batch, t, heads, args, in_specs):
    return pl.pallas_call(
        kernel,
        grid=(batch, heads),
        in_specs=in_specs,
        out_specs=_head_spec(t),
        out_shape=jax.ShapeDtypeStruct((batch * t, heads * HEAD_DIM), BF16),
        compiler_params=_params("parallel", "parallel"),
        name=name,
    )(*args)


def _tri_dot(tri, x):
    a, b, c = _split3(x)
    return _dot(tri, a) + _dot(tri, b) + _dot(tri, c)


CUMSUM_CHUNK = 256


def _cumsum_kernel(lf_ref, o_ref):
    t = lf_ref.shape[1]
    r = lax.broadcasted_iota(jnp.int32, (CUMSUM_CHUNK, CUMSUM_CHUNK), 0)
    c = lax.broadcasted_iota(jnp.int32, (CUMSUM_CHUNK, CUMSUM_CHUNK), 1)
    tri = (c <= r).astype(BF16)
    carry = jnp.zeros((1, lf_ref.shape[2]), F32)
    for j in range(t // CUMSUM_CHUNK):
        rows = slice(j * CUMSUM_CHUNK, (j + 1) * CUMSUM_CHUNK)
        cs = _tri_dot(tri, lf_ref[0, rows, :]) + carry
        o_ref[0, rows, :] = cs
        carry = cs[CUMSUM_CHUNK - 1:CUMSUM_CHUNK, :]


def _cumsum_time(lf):
    b, t, h = lf.shape
    return pl.pallas_call(
        _cumsum_kernel,
        grid=(b,),
        in_specs=[pl.BlockSpec((1, t, h), lambda i: (i, 0, 0))],
        out_specs=pl.BlockSpec((1, t, h), lambda i: (i, 0, 0)),
        out_shape=jax.ShapeDtypeStruct(lf.shape, F32),
        compiler_params=_params("parallel"),
        name="cumsum_time",
    )(lf)


def _suffix_kernel(pt_ref, lf_ref, new_ref, o_ref, carry_ref):
    @pl.when(pl.program_id(1) == 0)
    def _():
        carry_ref[...] = new_ref[0]

    page = lf_ref.shape[2]
    r = lax.broadcasted_iota(jnp.int32, (page, page), 0)
    c = lax.broadcasted_iota(jnp.int32, (page, page), 1)
    tri = (c > r).astype(BF16)
    lf = lf_ref[0, 0]
    o_ref[0, 0] = _tri_dot(tri, lf) + carry_ref[...]
    carry_ref[...] += jnp.sum(lf, axis=0, keepdims=True)


def _suffix_bias(cache_lf, layer, page_table, lf_new):
    bsz, n_pages = page_table.shape
    page, h = cache_lf.shape[2], cache_lf.shape[3]
    grid_spec = pltpu.PrefetchScalarGridSpec(
        num_scalar_prefetch=1,
        grid=(bsz, n_pages),
        in_specs=[pl.BlockSpec((1, 1, page, h), lambda b, j, pt: (layer, pt[b, n_pages - 1 - j], 0, 0)),
                  pl.BlockSpec((1, 1, h), lambda b, j, pt: (b, 0, 0))],
        out_specs=pl.BlockSpec((1, 1, page, h), lambda b, j, pt: (b, n_pages - 1 - j, 0, 0)),
        scratch_shapes=[pltpu.VMEM((1, h), F32)],
    )
    return pl.pallas_call(
        _suffix_kernel,
        grid_spec=grid_spec,
        out_shape=jax.ShapeDtypeStruct((bsz, n_pages, page, h), F32),
        compiler_params=_params("parallel", "arbitrary"),
        name="suffix_bias",
    )(page_table, cache_lf, lf_new.reshape(bsz, 1, h))


def _kmean_kernel(pt_ref, k0_ref, k1_ref, o_ref):
    tot = jnp.sum(k0_ref[0, 0], axis=0) + jnp.sum(k1_ref[0, 0], axis=0)
    o_ref[0, 0] = tot * (1.0 / MOBA_BLOCK)


def _kmean_pages(cache_k, layer, page_table):
    bsz, n_pages = page_table.shape
    page, h, d = cache_k.shape[2:]
    per = MOBA_BLOCK // page
    assert per == 2
    nblk = n_pages // per

    def spec(g):
        return pl.BlockSpec((1, 1, page, h, d), lambda b, n, pt: (layer, pt[b, per * n + g], 0, 0, 0))

    grid_spec = pltpu.PrefetchScalarGridSpec(
        num_scalar_prefetch=1,
        grid=(bsz, nblk),
        in_specs=[spec(0), spec(1)],
        out_specs=pl.BlockSpec((1, 1, h, d), lambda b, n, pt: (b, n, 0, 0)),
    )
    return pl.pallas_call(
        _kmean_kernel,
        grid_spec=grid_spec,
        out_shape=jax.ShapeDtypeStruct((bsz, nblk, h, d), F32),
        compiler_params=_params("parallel", "parallel"),
        name="kmean_pages",
    )(page_table, cache_k, cache_k)


def _moba_select_kernel(km_ref, q_ref, o_ref):
    km = km_ref[0]
    gate = jnp.sum(km * q_ref[...], axis=-1, keepdims=True)
    nblk = gate.shape[0]
    idx = lax.broadcasted_iota(jnp.int32, gate.shape, 0).astype(F32)
    sel = jnp.zeros(gate.shape, F32)
    for _ in range(min(MOBA_TOPK, nblk)):
        best = jnp.max(gate, axis=0, keepdims=True)
        first = jnp.min(jnp.where(gate == best, idx, float(nblk)), axis=0, keepdims=True)
        hit = idx == first
        sel = jnp.where(hit, 1.0, sel)
        gate = jnp.where(hit, -jnp.inf, gate)
    o_ref[0] = sel


def _moba_select(kmean, q):
    bsz, nblk, h, d = kmean.shape
    return pl.pallas_call(
        _moba_select_kernel,
        grid=(bsz,),
        in_specs=[pl.BlockSpec((1, nblk, h, d), lambda b: (b, 0, 0, 0)),
                  pl.BlockSpec((1, h, d), lambda b: (b, 0, 0))],
        out_specs=pl.BlockSpec((1, nblk, h, 1), lambda b: (b, 0, 0, 0)),
        out_shape=jax.ShapeDtypeStruct((bsz, nblk, h, 1), F32),
        compiler_params=_params("parallel"),
        name="moba_select",
    )(kmean, q)


def _decode_kernel(*refs, mode, n_pg, heads, scale, lam_init):
    pt_ref, q_ref = refs[0], refs[1]
    k_refs = refs[2:2 + n_pg]
    v_refs = refs[2 + n_pg:2 + 2 * n_pg]
    pos = 2 + 2 * n_pg
    if mode == "moba":
        sel_ref = refs[pos]
        pos += 1
    elif mode == "fox":
        bias_refs = refs[pos:pos + n_pg]
        pos += n_pg
    knew_ref, vnew_ref = refs[pos], refs[pos + 1]
    pos += 2
    if mode == "diff":
        lq1_ref, lk1_ref, lq2_ref, lk2_ref, g_ref = refs[pos:pos + 5]
        pos += 5
    o_ref, m_ref, l_ref, acc_ref = refs[pos:pos + 4]

    j = pl.program_id(1)

    @pl.when(j == 0)
    def _():
        m_ref[...] = jnp.full(m_ref.shape, NEG, F32)
        l_ref[...] = jnp.zeros(l_ref.shape, F32)
        acc_ref[...] = jnp.zeros(acc_ref.shape, F32)

    q = q_ref[0]
    nrow = q.shape[0]
    page = k_refs[0].shape[2]
    width = page * heads
    lane = lax.broadcasted_iota(jnp.int32, (nrow, width), 1)
    row = lax.broadcasted_iota(jnp.int32, (nrow, width), 0)
    assert heads & (heads - 1) == 0
    ok = (lane & (heads - 1)) == (row & (heads - 1))
    if mode == "moba":
        ok = ok & (sel_ref[0, 0] > 0.5)

    for g in range(n_pg):
        kf = k_refs[g][0, 0].reshape(width, HEAD_DIM).astype(BF16)
        vf = v_refs[g][0, 0].reshape(width, HEAD_DIM).astype(BF16)
        s = _dot_nt(q, kf) * scale
        if mode == "fox":
            s = s + bias_refs[g][0, 0]
        s = jnp.where(ok, s, NEG)
        m_old = m_ref[...]
        m_new = jnp.maximum(m_old, jnp.max(s, axis=-1, keepdims=True))
        alpha = jnp.exp(m_old - m_new)
        p = jnp.where(ok, jnp.exp(s - m_new), 0.0)
        l_ref[...] = alpha * l_ref[...] + jnp.sum(p, axis=-1, keepdims=True)
        acc_ref[...] = alpha * acc_ref[...] + _dot(p.astype(BF16), vf)
        m_ref[...] = m_new

    @pl.when(j == pl.num_programs(1) - 1)
    def _():
        reps = nrow // heads
        knew = knew_ref[0].astype(BF16).astype(F32)
        vnew = vnew_ref[0].astype(BF16).astype(F32)
        if reps > 1:
            knew = jnp.concatenate([knew] * reps, axis=0)
            vnew = jnp.concatenate([vnew] * reps, axis=0)
        s_new = jnp.sum(q.astype(F32) * knew, axis=-1, keepdims=True) * scale
        m_old = m_ref[...]
        m_new = jnp.maximum(m_old, s_new)
        alpha = jnp.exp(m_old - m_new)
        p_new = jnp.exp(s_new - m_new)
        l = alpha * l_ref[...] + p_new
        acc = alpha * acc_ref[...] + p_new.astype(BF16).astype(F32) * vnew
        o = acc / l
        if mode == "diff":
            lam = _diff_lambda(lq1_ref[...], lk1_ref[...], lq2_ref[...], lk2_ref[...], lam_init)
            o = _diff_finish(o[0:heads] - lam * o[heads:2 * heads], g_ref[...], lam_init)
        o_ref[0] = o.astype(o_ref.dtype)


def _decode_attn(mode, q, cache_k, cache_v, layer, page_table, k_new, v_new, *, n_pg, scale,
                 sel=None, bias=None, diff_params=None, lam_init=0.0):
    bsz, n_pages = page_table.shape
    page, heads, d = cache_k.shape[2:]
    nrow = q.shape[1]
    steps = n_pages // n_pg

    def page_spec(g):
        return pl.BlockSpec((1, 1, page, heads, d), lambda b, j, pt: (layer, pt[b, n_pg * j + g], 0, 0, 0))

    in_specs = [pl.BlockSpec((1, nrow, d), lambda b, j, pt: (b, 0, 0))]
    in_specs += [page_spec(g) for g in range(n_pg)] * 2
    args = [q] + [cache_k] * n_pg + [cache_v] * n_pg
    if mode == "moba":
        in_specs.append(pl.BlockSpec((1, 1, nrow, 1), lambda b, j, pt: (b, j, 0, 0)))
        args.append(sel)
    elif mode == "fox":
        for g in range(n_pg):
            in_specs.append(pl.BlockSpec((1, 1, 1, page * heads),
                                         functools.partial(lambda b, j, pt, g: (b, n_pg * j + g, 0, 0), g=g)))
        args += [bias] * n_pg
    in_specs += [pl.BlockSpec((1, heads, d), lambda b, j, pt: (b, 0, 0))] * 2
    args += [k_new, v_new]
    if mode == "diff":
        for p in diff_params:
            in_specs.append(pl.BlockSpec((1, p.shape[1]), lambda b, j, pt: (0, 0)))
        args += list(diff_params)
    grid_spec = pltpu.PrefetchScalarGridSpec(
        num_scalar_prefetch=1,
        grid=(bsz, steps),
        in_specs=in_specs,
        out_specs=pl.BlockSpec((1, heads, d), lambda b, j, pt: (b, 0, 0)),
        scratch_shapes=[pltpu.VMEM((nrow, 1), F32), pltpu.VMEM((nrow, 1), F32), pltpu.VMEM((nrow, d), F32)],
    )
    return pl.pallas_call(
        functools.partial(_decode_kernel, mode=mode, n_pg=n_pg, heads=heads, scale=scale, lam_init=lam_init),
        grid_spec=grid_spec,
        out_shape=jax.ShapeDtypeStruct((bsz, heads, d), BF16),
        compiler_params=_params("parallel", "arbitrary"),
        name="decode_" + mode,
    )(page_table, *args)


def _pad_rows(x, rows):
    return jnp.pad(x, ((0, rows - x.shape[0]), (0, 0)))


def _heads5(x2d, batch, t, heads):
    return x2d.reshape(1, batch, t, heads, HEAD_DIM)


def kernel(x_prompt, x_sample, cache_a_k, cache_a_v, cache_b_k, cache_b_v, cache_c_k, cache_c_v, cache_c_logf, page_table, p_prompt, p_sample, ffn1_norm, ffn1_w_gate, ffn1_w_up, ffn1_w_down, mix_norm, ab_w_in, b_lambda_q1, b_lambda_k1, b_lambda_q2, b_lambda_k2, b_subln, ab_w_out, c_w_in, c_b_f, c_w_out, ffn2_norm, ffn2_w_gate, ffn2_w_up, ffn2_w_down, ple_norm, ple_w_gate, ple_w_proj, final_norm):
    batch, seq, d_model = x_prompt.shape
    dec_batch, dec_seq, _ = x_sample.shape
    assert dec_seq == 1 and dec_batch <= SAMPLE_ROWS
    depth = ffn1_norm.shape[0]
    page = cache_a_k.shape[2]
    past_len = page_table.shape[1] * page
    a_heads, b_heads, c_heads = cache_a_k.shape[3], cache_b_k.shape[3], cache_c_k.shape[3]
    a_w, b_w, c_w = a_heads * HEAD_DIM, b_heads * HEAD_DIM, c_heads * HEAD_DIM
    m_p = batch * seq
    ple_dim = p_prompt.shape[-1]

    pos_p = jnp.arange(seq, dtype=jnp.int32)
    pos_s = jnp.full((SAMPLE_ROWS,), past_len, jnp.int32)
    rope_p = {k: _rope_tables(pos_p, k) for k in "ab"}
    rope_s = {k: _rope_tables(pos_s, k) for k in "ab"}

    w_bf = {name: _cast_bf16(w) for name, w in dict(
        ffn1_w_gate=ffn1_w_gate, ffn1_w_up=ffn1_w_up, ffn1_w_down=ffn1_w_down,
        ffn2_w_gate=ffn2_w_gate, ffn2_w_up=ffn2_w_up, ffn2_w_down=ffn2_w_down,
        ab_w_in=ab_w_in, ab_w_out=ab_w_out, c_w_in=c_w_in, c_w_out=c_w_out, ple_w_gate=ple_w_gate).items()}

    xp = x_prompt.reshape(m_p, d_model)
    xs = _pad_rows(x_sample.reshape(dec_batch, d_model), SAMPLE_ROWS)
    groups = [dict(x=xp, batch=batch, t=seq), dict(x=xs, batch=dec_batch, t=1)]
    outs = {k: [] for k in ("akp", "avp", "aks", "avs", "bkp", "bvp", "bks", "bvs",
                            "ckp", "cvp", "cfp", "cks", "cvs", "cfs")}

    def sample_heads(x2d, heads):
        return x2d[:dec_batch].reshape(dec_batch, heads, HEAD_DIM)

    for i in range(depth):
        for grp, p_in in zip(groups, (p_prompt, p_sample)):
            is_prompt = grp is groups[0]
            x = _ffn(grp["x"], ffn1_norm[i], w_bf["ffn1_w_gate"], w_bf["ffn1_w_up"], w_bf["ffn1_w_down"], i)
            h = _rmsnorm(x, mix_norm[i], BF16)
            rope = rope_p if is_prompt else rope_s
            if i % 2 == 0:
                e = i // 2
                lam_init = 0.8 - 0.6 * math.exp(-0.3 * i)
                w_in = w_bf["ab_w_in"]
                (qa,) = _proj(h, w_in, e, 0, a_w, [BF16], "a", rope["a"])
                ka, ka_bf = _proj(h, w_in, e, a_w, a_w, [F32, BF16], "a", rope["a"])
                va, va_bf = _proj(h, w_in, e, 2 * a_w, a_w, [F32, BF16])
                (qb,) = _proj(h, w_in, e, 3 * a_w, b_w, [BF16], "b", rope["b"])
                kb, kb_bf = _proj(h, w_in, e, 3 * a_w + b_w, b_w, [F32, BF16], "b", rope["b"])
                vb, vb_bf = _proj(h, w_in, e, 3 * a_w + 2 * b_w, b_w, [F32, BF16])
                diff_params = [v[e].reshape(1, -1) for v in (b_lambda_q1, b_lambda_k1, b_lambda_q2, b_lambda_k2,
                                                             b_subln)]
                if is_prompt:
                    hs = _head_spec(seq)
                    a_out = _prompt_attn(_moba_prompt_kernel, "moba_prompt", batch, seq, a_heads,
                                         [qa, ka_bf, va_bf, ka], [hs] * 4)
                    b_out = _prompt_attn(functools.partial(_diff_prompt_kernel, lam_init=lam_init), "diff_prompt",
                                         batch, seq, b_heads, [qb, kb_bf, vb_bf] + diff_params,
                                         [hs] * 3 + [_vec_spec(p.shape[1]) for p in diff_params])
                    for key, val, nh in (("akp", ka, a_heads), ("avp", va, a_heads),
                                         ("bkp", kb, b_heads), ("bvp", vb, b_heads)):
                        outs[key].append(val.reshape(batch, seq, nh, HEAD_DIM))
                else:
                    ka_h, va_h = sample_heads(ka, a_heads), sample_heads(va, a_heads)
                    kb_h, vb_h = sample_heads(kb, b_heads), sample_heads(vb, b_heads)
                    (qa_f32,) = _proj(h, w_in, e, 0, a_w, [F32], "a", rope["a"])
                    kmean = _kmean_pages(cache_a_k, e, page_table)
                    sel = _moba_select(kmean, sample_heads(qa_f32, a_heads))
                    a_o = _decode_attn("moba", sample_heads(qa, a_heads), cache_a_k, cache_a_v, e, page_table,
                                       ka_h, va_h, n_pg=MOBA_BLOCK // page, scale=HEAD_DIM ** -0.5,
                                       sel=sel)
                    qb_h = sample_heads(qb, b_heads)
                    lane = jnp.arange(HEAD_DIM) < HEAD_DIM // 2
                    qb2 = jnp.concatenate([jnp.where(lane, qb_h, 0), jnp.where(lane, 0, qb_h)], axis=1)
                    b_o = _decode_attn("diff", qb2, cache_b_k, cache_b_v, e, page_table, kb_h, vb_h,
                                       n_pg=4, scale=(HEAD_DIM // 2) ** -0.5, diff_params=diff_params,
                                       lam_init=lam_init)
                    a_out = _pad_rows(a_o.reshape(dec_batch, a_w), SAMPLE_ROWS)
                    b_out = _pad_rows(b_o.reshape(dec_batch, b_w), SAMPLE_ROWS)
                    for key, val in (("aks", ka_h), ("avs", va_h), ("bks", kb_h), ("bvs", vb_h)):
                        outs[key].append(val[:, None])
                x = _out_proj([a_out, b_out], w_bf["ab_w_out"], e, x)
            else:
                o = i // 2
                w_in = w_bf["c_w_in"]
                (q,) = _proj(h, w_in, o, 0, c_w, [BF16])
                k, k_bf = _proj(h, w_in, o, c_w, c_w, [F32, BF16])
                v, v_bf = _proj(h, w_in, o, 2 * c_w, c_w, [F32, BF16])
                w_f = jnp.pad(c_w_in[o][:, 3 * c_w:], ((0, 0), (0, LANES - c_heads)))
                b_f = jnp.pad(c_b_f[o], (0, LANES - c_heads)).reshape(1, LANES)
                logf = _logf_proj(h, w_f, b_f)[:, :c_heads]
                if is_prompt:
                    c = _cumsum_time(logf.reshape(batch, seq, c_heads)).transpose(0, 2, 1)
                    hs = _head_spec(seq)
                    c_out = _prompt_attn(
                        _fox_prompt_kernel, "fox_prompt", batch, seq, c_heads,
                        [q, k_bf, v_bf, c.reshape(batch, c_heads, 1, seq), c.reshape(batch, c_heads, seq, 1)],
                        [hs] * 3 + [pl.BlockSpec((1, 1, 1, seq), lambda b, hh: (b, hh, 0, 0)),
                                    pl.BlockSpec((1, 1, seq, 1), lambda b, hh: (b, hh, 0, 0))])
                    outs["ckp"].append(k.reshape(batch, seq, c_heads, HEAD_DIM))
                    outs["cvp"].append(v.reshape(batch, seq, c_heads, HEAD_DIM))
                    outs["cfp"].append(logf.reshape(batch, seq, c_heads))
                else:
                    k_h, v_h = sample_heads(k, c_heads), sample_heads(v, c_heads)
                    lf_new = logf[:dec_batch]
                    bias = _suffix_bias(cache_c_logf, o, page_table, lf_new)
                    bias = bias.reshape(dec_batch, page_table.shape[1], 1, page * c_heads)
                    c_o = _decode_attn("fox", sample_heads(q, c_heads), cache_c_k, cache_c_v, o, page_table,
                                       k_h, v_h, n_pg=2, scale=HEAD_DIM ** -0.5, bias=bias)
                    c_out = _pad_rows(c_o.reshape(dec_batch, c_w), SAMPLE_ROWS)
                    outs["cks"].append(k_h[:, None])
                    outs["cvs"].append(v_h[:, None])
                    outs["cfs"].append(lf_new[:, None])
                x = _out_proj([c_out], w_bf["c_w_out"], o, x)
            x = _ffn(x, ffn2_norm[i], w_bf["ffn2_w_gate"], w_bf["ffn2_w_up"], w_bf["ffn2_w_down"], i)
            hp = _rmsnorm(x, ple_norm[i], BF16)
            p2d = p_in[i].reshape(-1, ple_dim)
            if not is_prompt:
                p2d = _pad_rows(p2d, SAMPLE_ROWS)
            grp["x"] = _ple(hp, w_bf["ple_w_gate"], p2d, ple_w_proj, i, x)

    y_prompt = _rmsnorm(groups[0]["x"], final_norm, F32).reshape(batch, seq, d_model)
    y_sample = _rmsnorm(groups[1]["x"], final_norm, F32)[:dec_batch].reshape(dec_batch, dec_seq, d_model)
    st = {k: jnp.stack(v) for k, v in outs.items()}
    return (y_prompt, y_sample,
            st["akp"], st["avp"], st["aks"], st["avs"],
            st["bkp"], st["bvp"], st["bks"], st["bvs"],
            st["ckp"], st["cvp"], st["cfp"], st["cks"], st["cvs"], st["cfs"])
```

```python
import functools
import math

import jax
import jax.numpy as jnp
from jax import lax
from jax.experimental import pallas as pl
from jax.experimental.pallas import tpu as pltpu

F32 = jnp.float32
BF16 = jnp.bfloat16

HEAD_DIM = 128
MOBA_BLOCK = 256
MOBA_TOPK = 3
ROPE_THETA = 10000.0
NORM_EPS = 1e-6
SUBLN_EPS = 1e-5
NEG = -1e30

V7X_VMEM_BYTES = 64 * 1024 * 1024
VMEM_LIMIT = V7X_VMEM_BYTES - 8 * 1024 * 1024
LANES = 128
SAMPLE_ROWS = 16


def _params(*sem):
    return pltpu.CompilerParams(dimension_semantics=sem, vmem_limit_bytes=VMEM_LIMIT)


def _split3(x):
    a = x.astype(BF16)
    r = x - a.astype(F32)
    b = r.astype(BF16)
    c = (r - b.astype(F32)).astype(BF16)
    return a, b, c


def _dot(a, b):
    return jnp.dot(a, b, preferred_element_type=F32)


def _dot_nt(a, b):
    return lax.dot_general(a, b, (((1,), (1,)), ((), ())), preferred_element_type=F32)


def _cast_kernel(w_ref, o_ref):
    o_ref[...] = w_ref[...].astype(BF16)


def _cast_bf16(w):
    l, k, n = w.shape
    rows = 128
    return pl.pallas_call(
        _cast_kernel,
        grid=(l, k // rows),
        in_specs=[pl.BlockSpec((1, rows, n), lambda i, j: (i, j, 0))],
        out_specs=pl.BlockSpec((1, rows, n), lambda i, j: (i, j, 0)),
        out_shape=jax.ShapeDtypeStruct(w.shape, BF16),
        compiler_params=_params("parallel", "parallel"),
        name="cast_bf16",
    )(w)


def _rms(x, g, eps):
    return x * lax.rsqrt(jnp.mean(x * x, axis=-1, keepdims=True) + eps) * g


def _rmsnorm_kernel(x_ref, g_ref, o_ref):
    o_ref[...] = _rms(x_ref[...], g_ref[...], NORM_EPS).astype(o_ref.dtype)


def _rmsnorm(x, g, out_dtype):
    m, d = x.shape
    bm = min(512, m)
    return pl.pallas_call(
        _rmsnorm_kernel,
        grid=(m // bm,),
        in_specs=[pl.BlockSpec((bm, d), lambda i: (i, 0)),
                  pl.BlockSpec((1, d), lambda i: (0, 0))],
        out_specs=pl.BlockSpec((bm, d), lambda i: (i, 0)),
        out_shape=jax.ShapeDtypeStruct((m, d), out_dtype),
        compiler_params=_params("parallel"),
        name="rmsnorm",
    )(x, g.reshape(1, d))


FFN_SLAB = 256


def _ffn_kernel(x_ref, g_ref, wg_ref, wu_ref, wd_ref, o_ref, xn_ref):
    @pl.when(pl.program_id(1) == 0)
    def _():
        x = x_ref[...]
        xn_ref[...] = _rms(x, g_ref[...], NORM_EPS).astype(BF16)
        o_ref[...] = x

    xn = xn_ref[...]
    gate = _dot(xn, wg_ref[0])
    up = _dot(xn, wu_ref[0])
    act = (0.5 * (gate * jax.nn.sigmoid(gate)) * up).astype(BF16)
    o_ref[...] += _dot(act, wd_ref[0])


def _ffn(x, g, wg, wu, wd, layer):
    m, d = x.shape
    f = wg.shape[2]
    bm = min(512, m)
    return pl.pallas_call(
        _ffn_kernel,
        grid=(m // bm, f // FFN_SLAB),
        in_specs=[pl.BlockSpec((bm, d), lambda i, j: (i, 0), pipeline_mode=pl.Buffered(1)),
                  pl.BlockSpec((1, d), lambda i, j: (0, 0)),
                  pl.BlockSpec((1, d, FFN_SLAB), lambda i, j: (layer, 0, j)),
                  pl.BlockSpec((1, d, FFN_SLAB), lambda i, j: (layer, 0, j)),
                  pl.BlockSpec((1, FFN_SLAB, d), lambda i, j: (layer, j, 0))],
        out_specs=pl.BlockSpec((bm, d), lambda i, j: (i, 0)),
        out_shape=jax.ShapeDtypeStruct((m, d), F32),
        scratch_shapes=[pltpu.VMEM((bm, d), BF16)],
        compiler_params=_params("parallel", "arbitrary"),
        name="ffn",
    )(x, g.reshape(1, d), wg, wu, wd)


def _rope_tables(pos, kind):
    width = HEAD_DIM if kind == "a" else HEAD_DIM // 2
    half = width // 2
    inv_freq = jnp.power(ROPE_THETA, -jnp.arange(half, dtype=F32) / half)
    ang = pos.astype(F32)[:, None] * inv_freq[None, :]
    cos, sin = jnp.cos(ang), jnp.sin(ang)
    reps = HEAD_DIM // width
    cos_t = jnp.tile(jnp.concatenate([cos, cos], axis=-1), (1, reps))
    sin_t = jnp.tile(jnp.concatenate([-sin, sin], axis=-1), (1, reps))
    return cos_t, sin_t


def _rope_head(x, cos, sin, kind):
    if kind == "a":
        partner = pltpu.roll(x, HEAD_DIM // 2, 1)
    else:
        lane = lax.broadcasted_iota(jnp.int32, x.shape, 1)
        first = (lane & (HEAD_DIM // 2 - 1)) < (HEAD_DIM // 4)
        partner = jnp.where(first, pltpu.roll(x, HEAD_DIM - HEAD_DIM // 4, 1), pltpu.roll(x, HEAD_DIM // 4, 1))
    return x * cos + partner * sin


def _proj_kernel(*refs, rope, n_out):
    if rope is None:
        a_ref, w_ref = refs[:2]
        outs = refs[2:]
    else:
        a_ref, w_ref, cos_ref, sin_ref = refs[:4]
        outs = refs[4:]
    acc = _dot(a_ref[...], w_ref[0].astype(BF16))
    if rope is None:
        for o in outs[:n_out]:
            o[...] = acc.astype(o.dtype)
        return
    cos, sin = cos_ref[...], sin_ref[...]
    for h in range(acc.shape[1] // HEAD_DIM):
        cols = slice(h * HEAD_DIM, (h + 1) * HEAD_DIM)
        head = _rope_head(acc[:, cols], cos, sin, rope)
        for o in outs[:n_out]:
            o[:, cols] = head.astype(o.dtype)


def _proj(a, w, layer, col0, width, out_dtypes, rope=None, tables=None):
    m, k = a.shape
    bm = min(1024, m) if rope is None else min(1024, m, tables[0].shape[0])
    bn = min(512, width)
    assert col0 % bn == 0 and width % bn == 0
    in_specs = [pl.BlockSpec((bm, k), lambda i, j: (i, 0)),
                pl.BlockSpec((1, k, bn), lambda i, j: (layer, 0, col0 // bn + j))]
    args = [a, w]
    if rope is not None:
        nt = tables[0].shape[0] // bm
        in_specs += [pl.BlockSpec((bm, HEAD_DIM), lambda i, j: (i % nt, 0))] * 2
        args += list(tables)
    outs = pl.pallas_call(
        functools.partial(_proj_kernel, rope=rope, n_out=len(out_dtypes)),
        grid=(m // bm, width // bn),
        in_specs=in_specs,
        out_specs=[pl.BlockSpec((bm, bn), lambda i, j: (i, j))] * len(out_dtypes),
        out_shape=[jax.ShapeDtypeStruct((m, width), dt) for dt in out_dtypes],
        compiler_params=_params("parallel", "parallel"),
        name="proj_" + (rope or "plain"),
    )(*args)
    return outs


def _logf_kernel(a_ref, w_ref, b_ref, o_ref):
    z = _dot(a_ref[...], w_ref[...].astype(BF16)) + b_ref[...]
    o_ref[...] = jnp.minimum(z, 0.0) - jnp.log1p(jnp.exp(-jnp.abs(z)))


def _logf_proj(a, w_f, b_f):
    m, k = a.shape
    bm = min(1024, m)
    return pl.pallas_call(
        _logf_kernel,
        grid=(m // bm,),
        in_specs=[pl.BlockSpec((bm, k), lambda i: (i, 0)),
                  pl.BlockSpec((k, LANES), lambda i: (0, 0)),
                  pl.BlockSpec((1, LANES), lambda i: (0, 0))],
        out_specs=pl.BlockSpec((bm, LANES), lambda i: (i, 0)),
        out_shape=jax.ShapeDtypeStruct((m, LANES), F32),
        compiler_params=_params("parallel"),
        name="logf_proj",
    )(a, w_f, b_f)


def _out_proj_kernel(*refs, n_lhs):
    a_refs = refs[:n_lhs]
    w_ref, x_ref, o_ref = refs[n_lhs:]
    acc = x_ref[...]
    k0 = 0
    for a_ref in a_refs:
        kk = a_ref.shape[1]
        acc = acc + _dot(a_ref[...], w_ref[0, k0:k0 + kk, :])
        k0 += kk
    o_ref[...] = acc


def _out_proj(lhs, w, layer, x):
    m, n = x.shape
    k = w.shape[1]
    bm = min(1024, m)
    bn = min(512, n)
    in_specs = [pl.BlockSpec((bm, a.shape[1]), lambda i, j: (i, 0)) for a in lhs]
    in_specs += [pl.BlockSpec((1, k, bn), lambda i, j: (layer, 0, j)),
                 pl.BlockSpec((bm, bn), lambda i, j: (i, j))]
    return pl.pallas_call(
        functools.partial(_out_proj_kernel, n_lhs=len(lhs)),
        grid=(m // bm, n // bn),
        in_specs=in_specs,
        out_specs=pl.BlockSpec((bm, bn), lambda i, j: (i, j)),
        out_shape=jax.ShapeDtypeStruct((m, n), F32),
        compiler_params=_params("parallel", "parallel"),
        name="out_proj",
    )(*lhs, w, x)


def _ple_kernel(a_ref, wg_ref, p_ref, wp_ref, x_ref, o_ref):
    gate = jax.nn.sigmoid(_dot(a_ref[...], wg_ref[0]))
    emb = _dot(p_ref[...].astype(BF16), wp_ref[0].astype(BF16))
    o_ref[...] = x_ref[...] + gate * emb


def _ple(a, wg, p, wp, layer, x):
    m, n = x.shape
    k = a.shape[1]
    kp = p.shape[1]
    bm = min(1024, m)
    bn = min(512, n)
    return pl.pallas_call(
        _ple_kernel,
        grid=(m // bm, n // bn),
        in_specs=[pl.BlockSpec((bm, k), lambda i, j: (i, 0)),
                  pl.BlockSpec((1, k, bn), lambda i, j: (layer, 0, j)),
                  pl.BlockSpec((bm, kp), lambda i, j: (i, 0)),
                  pl.BlockSpec((1, kp, bn), lambda i, j: (layer, 0, j)),
                  pl.BlockSpec((bm, bn), lambda i, j: (i, j))],
        out_specs=pl.BlockSpec((bm, bn), lambda i, j: (i, j)),
        out_shape=jax.ShapeDtypeStruct((m, n), F32),
        compiler_params=_params("parallel", "parallel"),
        name="ple",
    )(a, wg, p, wp, x)


ATTN_TQ = 256


def _diag_mask():
    r = lax.broadcasted_iota(jnp.int32, (ATTN_TQ, ATTN_TQ), 0)
    c = lax.broadcasted_iota(jnp.int32, (ATTN_TQ, ATTN_TQ), 1)
    return c <= r


def _causal_softmax(t_raw, i, scale):
    kprev = i * ATTN_TQ
    c = scale * math.log2(math.e)
    t_d = jnp.where(_diag_mask(), t_raw[:, kprev:], NEG)
    m = jnp.max(t_d, axis=-1, keepdims=True)
    if i:
        t_p = t_raw[:, :kprev]
        m = jnp.maximum(m, jnp.max(t_p, axis=-1, keepdims=True))
    e_d = jnp.exp2((t_d - m) * c)
    tot = jnp.sum(e_d, axis=-1, keepdims=True)
    if i:
        e_p = jnp.exp2((t_p - m) * c)
        tot = tot + jnp.sum(e_p, axis=-1, keepdims=True)
    inv = 1.0 / tot
    return (e_p * inv if i else None), e_d * inv


def _pv(p_prev, p_diag, v_ref, i):
    kprev = i * ATTN_TQ
    o = _dot(p_diag.astype(BF16), v_ref[kprev:kprev + ATTN_TQ, :])
    if i:
        o = o + _dot(p_prev.astype(BF16), v_ref[0:kprev, :])
    return o


def _moba_prompt_kernel(q_ref, k_ref, v_ref, kf_ref, o_ref):
    t = q_ref.shape[0]
    nb = t // MOBA_BLOCK
    assert ATTN_TQ == MOBA_BLOCK and nb <= LANES
    scale = HEAD_DIM ** -0.5
    km = jnp.sum(kf_ref[...].reshape(nb, MOBA_BLOCK, HEAD_DIM), axis=1) * (1.0 / MOBA_BLOCK)
    km = jnp.concatenate([km, jnp.zeros((LANES - nb, HEAD_DIM), F32)], axis=0)
    km_hi = km.astype(BF16)
    km_lo = (km - km_hi.astype(F32)).astype(BF16)
    key_blk = lax.broadcasted_iota(jnp.int32, (t, LANES), 0) >> (MOBA_BLOCK.bit_length() - 1)
    onehot = (key_blk == lax.broadcasted_iota(jnp.int32, (t, LANES), 1)).astype(F32).astype(BF16)
    k_aug = jnp.concatenate([k_ref[...], onehot], axis=1)
    for i in range(nb):
        rows = slice(i * ATTN_TQ, (i + 1) * ATTN_TQ)
        kend = (i + 1) * MOBA_BLOCK
        q = q_ref[rows, :]
        if i > MOBA_TOPK:
            gate = _dot_nt(q, km_hi) + _dot_nt(q, km_lo)
            lane = lax.broadcasted_iota(jnp.int32, gate.shape, 1)
            rank = jnp.zeros(gate.shape, F32)
            for m in range(i):
                g_m = gate[:, m:m + 1]
                ahead = (g_m > gate) | ((g_m == gate) & (lane > m))
                rank = rank + ahead.astype(F32)
            bias = jnp.where((rank < MOBA_TOPK) | (lane >= i), 0.0, NEG)
            q_aug = jnp.concatenate([q, bias.astype(BF16)], axis=1)
            t_raw = _dot_nt(q_aug, k_aug[0:kend, :])
        else:
            t_raw = _dot_nt(q, k_ref[0:kend, :])
        p_prev, p_diag = _causal_softmax(t_raw, i, scale)
        o_ref[rows, :] = _pv(p_prev, p_diag, v_ref, i).astype(o_ref.dtype)


def _diff_lambda(lq1, lk1, lq2, lk2, lam_init):
    return (jnp.exp(jnp.sum(lq1 * lk1, axis=-1, keepdims=True))
            - jnp.exp(jnp.sum(lq2 * lk2, axis=-1, keepdims=True)) + lam_init)


def _diff_finish(o, subln, lam_init):
    return _rms(o, subln, SUBLN_EPS) * (1.0 - lam_init)


def _diff_prompt_kernel(q_ref, k_ref, v_ref, lq1_ref, lk1_ref, lq2_ref, lk2_ref, g_ref, o_ref, *, lam_init):
    t = q_ref.shape[0]
    half = HEAD_DIM // 2
    scale = half ** -0.5
    lam = _diff_lambda(lq1_ref[...], lk1_ref[...], lq2_ref[...], lk2_ref[...], lam_init)
    lane = lax.broadcasted_iota(jnp.int32, (ATTN_TQ, HEAD_DIM), 1)
    for i in range(t // ATTN_TQ):
        rows = slice(i * ATTN_TQ, (i + 1) * ATTN_TQ)
        kend = (i + 1) * ATTN_TQ
        q = q_ref[rows, :]
        k = k_ref[0:kend, :]
        p1_prev, p1_diag = _causal_softmax(_dot_nt(jnp.where(lane < half, q, jnp.zeros_like(q)), k), i, scale)
        p2_prev, p2_diag = _causal_softmax(_dot_nt(jnp.where(lane >= half, q, jnp.zeros_like(q)), k), i, scale)
        o = _pv((p1_prev - lam * p2_prev) if i else None, p1_diag - lam * p2_diag, v_ref, i)
        o_ref[rows, :] = _diff_finish(o, g_ref[...], lam_init).astype(o_ref.dtype)


def _fox_prompt_kernel(q_ref, k_ref, v_ref, ccol_ref, o_ref):
    t = q_ref.shape[0]
    scale = HEAD_DIM ** -0.5
    hi, mid, lo = (term.astype(F32) for term in _split3(ccol_ref[0, 0] * (1.0 / scale)))
    lane = lax.broadcasted_iota(jnp.int32, (t, LANES), 1)
    one = jnp.ones((t, LANES), F32)

    def features(vals):
        out = jnp.zeros((t, LANES), F32)
        for n, val in enumerate(vals):
            out = jnp.where(lane == n, val, out)
        return out.astype(BF16)

    q_feat = features([hi, mid, lo, one, one, one])
    k_aug = jnp.concatenate([k_ref[...], features([one, one, one, -hi, -mid, -lo])], axis=1)
    for i in range(t // ATTN_TQ):
        rows = slice(i * ATTN_TQ, (i + 1) * ATTN_TQ)
        kend = (i + 1) * ATTN_TQ
        q_aug = jnp.concatenate([q_ref[rows, :], q_feat[rows, :]], axis=1)
        p_prev, p_diag = _causal_softmax(_dot_nt(q_aug, k_aug[0:kend, :]), i, scale)
        o_ref[rows, :] = _pv(p_prev, p_diag, v_ref, i).astype(o_ref.dtype)


def _head_spec(t):
    return pl.BlockSpec((t, HEAD_DIM), lambda b, h: (b, h))


def _vec_spec(n):
    return pl.BlockSpec((1, n), lambda b, h: (0, 0))


def _prompt_attn(kernel, name, batch, t, heads, args, in_specs):
    return pl.pallas_call(
        kernel,
        grid=(batch, heads),
        in_specs=in_specs,
        out_specs=_head_spec(t),
        out_shape=jax.ShapeDtypeStruct((batch * t, heads * HEAD_DIM), BF16),
        compiler_params=_params("parallel", "parallel"),
        name=name,
    )(*args)


def _tri_dot(tri, x):
    a, b, c = _split3(x)
    return _dot(tri, a) + _dot(tri, b) + _dot(tri, c)


CUMSUM_CHUNK = 256


def _cumsum_kernel(lf_ref, o_ref):
    t = lf_ref.shape[1]
    r = lax.broadcasted_iota(jnp.int32, (CUMSUM_CHUNK, CUMSUM_CHUNK), 0)
    c = lax.broadcasted_iota(jnp.int32, (CUMSUM_CHUNK, CUMSUM_CHUNK), 1)
    tri = (c <= r).astype(BF16)
    carry = jnp.zeros((1, lf_ref.shape[2]), F32)
    for j in range(t // CUMSUM_CHUNK):
        rows = slice(j * CUMSUM_CHUNK, (j + 1) * CUMSUM_CHUNK)
        cs = _tri_dot(tri, lf_ref[0, rows, :]) + carry
        o_ref[0, rows, :] = cs
        carry = cs[CUMSUM_CHUNK - 1:CUMSUM_CHUNK, :]


def _cumsum_time(lf):
    b, t, h = lf.shape
    return pl.pallas_call(
        _cumsum_kernel,
        grid=(b,),
        in_specs=[pl.BlockSpec((1, t, h), lambda i: (i, 0, 0))],
        out_specs=pl.BlockSpec((1, t, h), lambda i: (i, 0, 0)),
        out_shape=jax.ShapeDtypeStruct(lf.shape, F32),
        compiler_params=_params("parallel"),
        name="cumsum_time",
    )(lf)


SUFFIX_PAGES = 8


def _suffix_kernel(*refs, n_pg):
    lf_refs = refs[1:1 + n_pg]
    new_ref, o_ref, carry_ref = refs[1 + n_pg:]

    @pl.when(pl.program_id(1) == 0)
    def _():
        carry_ref[...] = new_ref[0]

    page = lf_refs[0].shape[2]
    r = lax.broadcasted_iota(jnp.int32, (page, page), 0)
    c = lax.broadcasted_iota(jnp.int32, (page, page), 1)
    tri = (c > r).astype(BF16)
    carry = carry_ref[...]
    for g in reversed(range(n_pg)):
        lf = lf_refs[g][0, 0]
        o_ref[0, g] = _tri_dot(tri, lf) + carry
        carry = carry + jnp.sum(lf, axis=0, keepdims=True)
    carry_ref[...] = carry


def _suffix_bias(cache_lf, layer, page_table, lf_new):
    bsz, n_pages = page_table.shape
    page, h = cache_lf.shape[2], cache_lf.shape[3]
    n_pg = math.gcd(SUFFIX_PAGES, n_pages)
    steps = n_pages // n_pg

    def spec(g):
        return pl.BlockSpec((1, 1, page, h), lambda b, j, pt: (layer, pt[b, (steps - 1 - j) * n_pg + g], 0, 0))

    grid_spec = pltpu.PrefetchScalarGridSpec(
        num_scalar_prefetch=1,
        grid=(bsz, steps),
        in_specs=[spec(g) for g in range(n_pg)] + [pl.BlockSpec((1, 1, h), lambda b, j, pt: (b, 0, 0))],
        out_specs=pl.BlockSpec((1, n_pg, page, h), lambda b, j, pt: (b, steps - 1 - j, 0, 0)),
        scratch_shapes=[pltpu.VMEM((1, h), F32)],
    )
    return pl.pallas_call(
        functools.partial(_suffix_kernel, n_pg=n_pg),
        grid_spec=grid_spec,
        out_shape=jax.ShapeDtypeStruct((bsz, n_pages, page, h), F32),
        compiler_params=_params("parallel", "arbitrary"),
        name="suffix_bias",
    )(page_table, *([cache_lf] * n_pg), lf_new.reshape(bsz, 1, h))


KMEAN_BLOCKS = 4


def _kmean_kernel(*refs, per, blocks):
    k_refs, o_ref = refs[1:-1], refs[-1]
    for n in range(blocks):
        tot = jnp.sum(k_refs[per * n][0, 0], axis=0)
        for g in range(1, per):
            tot = tot + jnp.sum(k_refs[per * n + g][0, 0], axis=0)
        o_ref[0, n] = tot * (1.0 / MOBA_BLOCK)


def _kmean_pages(cache_k, layer, page_table):
    bsz, n_pages = page_table.shape
    page, h, d = cache_k.shape[2:]
    per = MOBA_BLOCK // page
    nblk = n_pages // per
    blocks = math.gcd(KMEAN_BLOCKS, nblk)
    n_pg = per * blocks

    def spec(g):
        return pl.BlockSpec((1, 1, page, h, d), lambda b, n, pt: (layer, pt[b, n_pg * n + g], 0, 0, 0))

    grid_spec = pltpu.PrefetchScalarGridSpec(
        num_scalar_prefetch=1,
        grid=(bsz, nblk // blocks),
        in_specs=[spec(g) for g in range(n_pg)],
        out_specs=pl.BlockSpec((1, blocks, h, d), lambda b, n, pt: (b, n, 0, 0)),
    )
    return pl.pallas_call(
        functools.partial(_kmean_kernel, per=per, blocks=blocks),
        grid_spec=grid_spec,
        out_shape=jax.ShapeDtypeStruct((bsz, nblk, h, d), F32),
        compiler_params=_params("parallel", "parallel"),
        name="kmean_pages",
    )(page_table, *([cache_k] * n_pg))


def _moba_select_kernel(km_ref, q_ref, o_ref):
    km = km_ref[0]
    gate = jnp.sum(km * q_ref[...], axis=-1, keepdims=True)
    nblk = gate.shape[0]
    idx = lax.broadcasted_iota(jnp.int32, gate.shape, 0).astype(F32)
    for s in range(MOBA_TOPK):
        best = jnp.max(gate, axis=0, keepdims=True)
        first = jnp.min(jnp.where(gate == best, idx, float(nblk)), axis=0, keepdims=True)
        o_ref[0, s] = first[0]
        gate = jnp.where(idx == first, -jnp.inf, gate)


def _moba_select(kmean, q):
    bsz, nblk, h, d = kmean.shape
    assert nblk >= MOBA_TOPK
    picks = pl.pallas_call(
        _moba_select_kernel,
        grid=(bsz,),
        in_specs=[pl.BlockSpec((1, nblk, h, d), lambda b: (b, 0, 0, 0)),
                  pl.BlockSpec((1, h, d), lambda b: (b, 0, 0))],
        out_specs=pl.BlockSpec((1, MOBA_TOPK, h, 1), lambda b: (b, 0, 0, 0)),
        out_shape=jax.ShapeDtypeStruct((bsz, MOBA_TOPK, h, 1), F32),
        compiler_params=_params("parallel"),
        name="moba_select",
    )(kmean, q)
    return picks[..., 0].astype(jnp.int32)


def _moba_decode_kernel(pt_ref, sel_ref, q_ref, k0_ref, k1_ref, v0_ref, v1_ref, knew_ref, vnew_ref, o_ref,
                        m_ref, l_ref, acc_ref, *, scale):
    s_id = pl.program_id(2)

    @pl.when(s_id == 0)
    def _():
        m_ref[...] = jnp.full(m_ref.shape, NEG, F32)
        l_ref[...] = jnp.zeros(l_ref.shape, F32)
        acc_ref[...] = jnp.zeros(acc_ref.shape, F32)

    q = q_ref[0, 0]
    page, grp = k0_ref.shape[2], k0_ref.shape[3]
    width = page * grp
    lane = lax.broadcasted_iota(jnp.int32, (1, width), 1)
    mine = (lane & (grp - 1)) == (pl.program_id(1) & (grp - 1))

    def update(s, ok, pv):
        m_old = m_ref[...]
        m_new = jnp.maximum(m_old, jnp.max(s, axis=-1, keepdims=True))
        alpha = jnp.exp(m_old - m_new)
        p = jnp.exp(s - m_new)
        if ok is not None:
            p = jnp.where(ok, p, 0.0)
        l_ref[...] = alpha * l_ref[...] + jnp.sum(p, axis=-1, keepdims=True)
        acc_ref[...] = alpha * acc_ref[...] + pv(p.astype(BF16))
        m_ref[...] = m_new

    for k_ref, v_ref in ((k0_ref, v0_ref), (k1_ref, v1_ref)):
        kf = k_ref[0, 0].reshape(width, HEAD_DIM).astype(BF16)
        update(jnp.where(mine, _dot_nt(q, kf) * scale, NEG), mine,
               lambda p, v_ref=v_ref: _dot(p, v_ref[0, 0].reshape(width, HEAD_DIM).astype(BF16)))

    @pl.when(s_id == pl.num_programs(2) - 1)
    def _():
        knew = knew_ref[0, 0].astype(BF16).astype(F32)
        vnew = vnew_ref[0, 0].astype(BF16).astype(F32)
        update(jnp.sum(q.astype(F32) * knew, axis=-1, keepdims=True) * scale, None, lambda p: p.astype(F32) * vnew)
        o_ref[0, 0] = (acc_ref[...] / l_ref[...]).astype(o_ref.dtype)


MOBA_HEAD_GROUP = 8


def _moba_decode(q, cache_k, cache_v, layer, page_table, picks, k_new, v_new):
    bsz, heads, d = q.shape
    page = cache_k.shape[2]
    per = MOBA_BLOCK // page
    grp = min(MOBA_HEAD_GROUP, heads)
    assert per == 2 and heads % grp == 0

    def page_spec(g):
        return pl.BlockSpec((1, 1, page, grp, d),
                            lambda b, h, s, pt, sel: (layer, pt[b, per * sel[b, s, h] + g], 0, h // grp, 0))

    row = pl.BlockSpec((1, 1, 1, d), lambda b, h, s, pt, sel: (b, h, 0, 0))
    grid_spec = pltpu.PrefetchScalarGridSpec(
        num_scalar_prefetch=2,
        grid=(bsz, heads, picks.shape[1]),
        in_specs=[row, page_spec(0), page_spec(1), page_spec(0), page_spec(1), row, row],
        out_specs=row,
        scratch_shapes=[pltpu.VMEM((1, 1), F32), pltpu.VMEM((1, 1), F32), pltpu.VMEM((1, d), F32)],
    )
    out = pl.pallas_call(
        functools.partial(_moba_decode_kernel, scale=HEAD_DIM ** -0.5),
        grid_spec=grid_spec,
        out_shape=jax.ShapeDtypeStruct((bsz, heads, 1, d), BF16),
        compiler_params=_params("parallel", "parallel", "arbitrary"),
        name="decode_moba",
    )(page_table, picks, q.reshape(bsz, heads, 1, d), cache_k, cache_k, cache_v, cache_v,
      k_new.reshape(bsz, heads, 1, d), v_new.reshape(bsz, heads, 1, d))
    return out.reshape(bsz, heads, d)


def _decode_kernel(*refs, mode, n_pg, heads, scale, lam_init):
    pt_ref, q_ref = refs[0], refs[1]
    k_refs = refs[2:2 + n_pg]
    v_refs = refs[2 + n_pg:2 + 2 * n_pg]
    pos = 2 + 2 * n_pg
    if mode == "fox":
        bias_refs = refs[pos:pos + n_pg]
        pos += n_pg
    knew_ref, vnew_ref = refs[pos], refs[pos + 1]
    pos += 2
    if mode == "diff":
        lq1_ref, lk1_ref, lq2_ref, lk2_ref, g_ref = refs[pos:pos + 5]
        pos += 5
    o_ref, m_ref, l_ref, acc_ref = refs[pos:pos + 4]

    j = pl.program_id(1)

    @pl.when(j == 0)
    def _():
        m_ref[...] = jnp.full(m_ref.shape, NEG, F32)
        l_ref[...] = jnp.zeros(l_ref.shape, F32)
        acc_ref[...] = jnp.zeros(acc_ref.shape, F32)

    q = q_ref[0]
    nrow = q.shape[0]
    page = k_refs[0].shape[2]
    width = page * heads
    lane = lax.broadcasted_iota(jnp.int32, (nrow, width), 1)
    row = lax.broadcasted_iota(jnp.int32, (nrow, width), 0)
    assert heads & (heads - 1) == 0
    ok = (lane & (heads - 1)) == (row & (heads - 1))

    scores = []
    for g in range(n_pg):
        kf = k_refs[g][0, 0].reshape(width, HEAD_DIM).astype(BF16)
        s = _dot_nt(q, kf) * scale
        if mode == "fox":
            s = s + bias_refs[g][0, 0]
        scores.append(jnp.where(ok, s, NEG))
    m_old = m_ref[...]
    m_new = m_old
    for s in scores:
        m_new = jnp.maximum(m_new, jnp.max(s, axis=-1, keepdims=True))
    alpha = jnp.exp(m_old - m_new)
    l = alpha * l_ref[...]
    acc = alpha * acc_ref[...]
    for g, s in enumerate(scores):
        p = jnp.where(ok, jnp.exp(s - m_new), 0.0)
        l = l + jnp.sum(p, axis=-1, keepdims=True)
        acc = acc + _dot(p.astype(BF16), v_refs[g][0, 0].reshape(width, HEAD_DIM).astype(BF16))
    m_ref[...] = m_new
    l_ref[...] = l
    acc_ref[...] = acc

    @pl.when(j == pl.num_programs(1) - 1)
    def _():
        reps = nrow // heads
        knew = knew_ref[0].astype(BF16).astype(F32)
        vnew = vnew_ref[0].astype(BF16).astype(F32)
        if reps > 1:
            knew = jnp.concatenate([knew] * reps, axis=0)
            vnew = jnp.concatenate([vnew] * reps, axis=0)
        s_new = jnp.sum(q.astype(F32) * knew, axis=-1, keepdims=True) * scale
        m_old = m_ref[...]
        m_new = jnp.maximum(m_old, s_new)
        alpha = jnp.exp(m_old - m_new)
        p_new = jnp.exp(s_new - m_new)
        l = alpha * l_ref[...] + p_new
        acc = alpha * acc_ref[...] + p_new.astype(BF16).astype(F32) * vnew
        o = acc / l
        if mode == "diff":
            lam = _diff_lambda(lq1_ref[...], lk1_ref[...], lq2_ref[...], lk2_ref[...], lam_init)
            o = _diff_finish(o[0:heads] - lam * o[heads:2 * heads], g_ref[...], lam_init)
        o_ref[0] = o.astype(o_ref.dtype)


def _decode_attn(mode, q, cache_k, cache_v, layer, page_table, k_new, v_new, *, n_pg, scale,
                 bias=None, diff_params=None, lam_init=0.0):
    bsz, n_pages = page_table.shape
    page, heads, d = cache_k.shape[2:]
    nrow = q.shape[1]
    n_pg = math.gcd(n_pg, n_pages)
    steps = n_pages // n_pg

    def page_spec(g):
        return pl.BlockSpec((1, 1, page, heads, d), lambda b, j, pt: (layer, pt[b, n_pg * j + g], 0, 0, 0))

    in_specs = [pl.BlockSpec((1, nrow, d), lambda b, j, pt: (b, 0, 0))]
    in_specs += [page_spec(g) for g in range(n_pg)] * 2
    args = [q] + [cache_k] * n_pg + [cache_v] * n_pg
    if mode == "fox":
        for g in range(n_pg):
            in_specs.append(pl.BlockSpec((1, 1, 1, page * heads),
                                         functools.partial(lambda b, j, pt, g: (b, n_pg * j + g, 0, 0), g=g)))
        args += [bias] * n_pg
    in_specs += [pl.BlockSpec((1, heads, d), lambda b, j, pt: (b, 0, 0))] * 2
    args += [k_new, v_new]
    if mode == "diff":
        for p in diff_params:
            in_specs.append(pl.BlockSpec((1, p.shape[1]), lambda b, j, pt: (0, 0)))
        args += list(diff_params)
    grid_spec = pltpu.PrefetchScalarGridSpec(
        num_scalar_prefetch=1,
        grid=(bsz, steps),
        in_specs=in_specs,
        out_specs=pl.BlockSpec((1, heads, d), lambda b, j, pt: (b, 0, 0)),
        scratch_shapes=[pltpu.VMEM((nrow, 1), F32), pltpu.VMEM((nrow, 1), F32), pltpu.VMEM((nrow, d), F32)],
    )
    return pl.pallas_call(
        functools.partial(_decode_kernel, mode=mode, n_pg=n_pg, heads=heads, scale=scale, lam_init=lam_init),
        grid_spec=grid_spec,
        out_shape=jax.ShapeDtypeStruct((bsz, heads, d), BF16),
        compiler_params=_params("parallel", "arbitrary"),
        name="decode_" + mode,
    )(page_table, *args)


def _pad_rows(x, rows):
    return jnp.pad(x, ((0, rows - x.shape[0]), (0, 0)))


def _heads5(x2d, batch, t, heads):
    return x2d.reshape(1, batch, t, heads, HEAD_DIM)


def kernel(x_prompt, x_sample, cache_a_k, cache_a_v, cache_b_k, cache_b_v, cache_c_k, cache_c_v, cache_c_logf, page_table, p_prompt, p_sample, ffn1_norm, ffn1_w_gate, ffn1_w_up, ffn1_w_down, mix_norm, ab_w_in, b_lambda_q1, b_lambda_k1, b_lambda_q2, b_lambda_k2, b_subln, ab_w_out, c_w_in, c_b_f, c_w_out, ffn2_norm, ffn2_w_gate, ffn2_w_up, ffn2_w_down, ple_norm, ple_w_gate, ple_w_proj, final_norm):
    batch, seq, d_model = x_prompt.shape
    dec_batch, dec_seq, _ = x_sample.shape
    assert dec_seq == 1 and dec_batch <= SAMPLE_ROWS
    depth = ffn1_norm.shape[0]
    page = cache_a_k.shape[2]
    past_len = page_table.shape[1] * page
    a_heads, b_heads, c_heads = cache_a_k.shape[3], cache_b_k.shape[3], cache_c_k.shape[3]
    a_w, b_w, c_w = a_heads * HEAD_DIM, b_heads * HEAD_DIM, c_heads * HEAD_DIM
    m_p = batch * seq
    ple_dim = p_prompt.shape[-1]

    pos_p = jnp.arange(seq, dtype=jnp.int32)
    pos_s = jnp.full((SAMPLE_ROWS,), past_len, jnp.int32)
    rope_p = {k: _rope_tables(pos_p, k) for k in "ab"}
    rope_s = {k: _rope_tables(pos_s, k) for k in "ab"}

    w_bf = {name: _cast_bf16(w) for name, w in dict(
        ffn1_w_gate=ffn1_w_gate, ffn1_w_up=ffn1_w_up, ffn1_w_down=ffn1_w_down,
        ffn2_w_gate=ffn2_w_gate, ffn2_w_up=ffn2_w_up, ffn2_w_down=ffn2_w_down,
        ab_w_in=ab_w_in, ab_w_out=ab_w_out, c_w_in=c_w_in, c_w_out=c_w_out, ple_w_gate=ple_w_gate).items()}

    xp = x_prompt.reshape(m_p, d_model)
    xs = _pad_rows(x_sample.reshape(dec_batch, d_model), SAMPLE_ROWS)
    groups = [dict(x=xp, batch=batch, t=seq), dict(x=xs, batch=dec_batch, t=1)]
    outs = {k: [] for k in ("akp", "avp", "aks", "avs", "bkp", "bvp", "bks", "bvs",
                            "ckp", "cvp", "cfp", "cks", "cvs", "cfs")}

    def sample_heads(x2d, heads):
        return x2d[:dec_batch].reshape(dec_batch, heads, HEAD_DIM)

    for i in range(depth):
        for grp, p_in in zip(groups, (p_prompt, p_sample)):
            is_prompt = grp is groups[0]
            x = _ffn(grp["x"], ffn1_norm[i], w_bf["ffn1_w_gate"], w_bf["ffn1_w_up"], w_bf["ffn1_w_down"], i)
            h = _rmsnorm(x, mix_norm[i], BF16)
            rope = rope_p if is_prompt else rope_s
            if i % 2 == 0:
                e = i // 2
                lam_init = 0.8 - 0.6 * math.exp(-0.3 * i)
                w_in = w_bf["ab_w_in"]
                (qa,) = _proj(h, w_in, e, 0, a_w, [BF16], "a", rope["a"])
                ka, ka_bf = _proj(h, w_in, e, a_w, a_w, [F32, BF16], "a", rope["a"])
                va, va_bf = _proj(h, w_in, e, 2 * a_w, a_w, [F32, BF16])
                (qb,) = _proj(h, w_in, e, 3 * a_w, b_w, [BF16], "b", rope["b"])
                kb, kb_bf = _proj(h, w_in, e, 3 * a_w + b_w, b_w, [F32, BF16], "b", rope["b"])
                vb, vb_bf = _proj(h, w_in, e, 3 * a_w + 2 * b_w, b_w, [F32, BF16])
                diff_params = [v[e].reshape(1, -1) for v in (b_lambda_q1, b_lambda_k1, b_lambda_q2, b_lambda_k2,
                                                             b_subln)]
                if is_prompt:
                    hs = _head_spec(seq)
                    a_out = _prompt_attn(_moba_prompt_kernel, "moba_prompt", batch, seq, a_heads,
                                         [qa, ka_bf, va_bf, ka], [hs] * 4)
                    b_out = _prompt_attn(functools.partial(_diff_prompt_kernel, lam_init=lam_init), "diff_prompt",
                                         batch, seq, b_heads, [qb, kb_bf, vb_bf] + diff_params,
                                         [hs] * 3 + [_vec_spec(p.shape[1]) for p in diff_params])
                    for key, val, nh in (("akp", ka, a_heads), ("avp", va, a_heads),
                                         ("bkp", kb, b_heads), ("bvp", vb, b_heads)):
                        outs[key].append(val.reshape(batch, seq, nh, HEAD_DIM))
                else:
                    ka_h, va_h = sample_heads(ka, a_heads), sample_heads(va, a_heads)
                    kb_h, vb_h = sample_heads(kb, b_heads), sample_heads(vb, b_heads)
                    (qa_f32,) = _proj(h, w_in, e, 0, a_w, [F32], "a", rope["a"])
                    kmean = _kmean_pages(cache_a_k, e, page_table)
                    picks = _moba_select(kmean, sample_heads(qa_f32, a_heads))
                    a_o = _moba_decode(sample_heads(qa, a_heads), cache_a_k, cache_a_v, e, page_table, picks,
                                       ka_h, va_h)
                    qb_h = sample_heads(qb, b_heads)
                    lane = jnp.arange(HEAD_DIM) < HEAD_DIM // 2
                    qb2 = jnp.concatenate([jnp.where(lane, qb_h, 0), jnp.where(lane, 0, qb_h)], axis=1)
                    b_o = _decode_attn("diff", qb2, cache_b_k, cache_b_v, e, page_table, kb_h, vb_h,
                                       n_pg=8, scale=(HEAD_DIM // 2) ** -0.5, diff_params=diff_params,
                                       lam_init=lam_init)
                    a_out = _pad_rows(a_o.reshape(dec_batch, a_w), SAMPLE_ROWS)
                    b_out = _pad_rows(b_o.reshape(dec_batch, b_w), SAMPLE_ROWS)
                    for key, val in (("aks", ka_h), ("avs", va_h), ("bks", kb_h), ("bvs", vb_h)):
                        outs[key].append(val[:, None])
                x = _out_proj([a_out, b_out], w_bf["ab_w_out"], e, x)
            else:
                o = i // 2
                w_in = w_bf["c_w_in"]
                (q,) = _proj(h, w_in, o, 0, c_w, [BF16])
                k, k_bf = _proj(h, w_in, o, c_w, c_w, [F32, BF16])
                v, v_bf = _proj(h, w_in, o, 2 * c_w, c_w, [F32, BF16])
                w_f = jnp.pad(c_w_in[o][:, 3 * c_w:], ((0, 0), (0, LANES - c_heads)))
                b_f = jnp.pad(c_b_f[o], (0, LANES - c_heads)).reshape(1, LANES)
                logf = _logf_proj(h, w_f, b_f)[:, :c_heads]
                if is_prompt:
                    c = _cumsum_time(logf.reshape(batch, seq, c_heads)).transpose(0, 2, 1)
                    hs = _head_spec(seq)
                    c_out = _prompt_attn(
                        _fox_prompt_kernel, "fox_prompt", batch, seq, c_heads,
                        [q, k_bf, v_bf, c.reshape(batch, c_heads, seq, 1)],
                        [hs] * 3 + [pl.BlockSpec((1, 1, seq, 1), lambda b, hh: (b, hh, 0, 0))])
                    outs["ckp"].append(k.reshape(batch, seq, c_heads, HEAD_DIM))
                    outs["cvp"].append(v.reshape(batch, seq, c_heads, HEAD_DIM))
                    outs["cfp"].append(logf.reshape(batch, seq, c_heads))
                else:
                    k_h, v_h = sample_heads(k, c_heads), sample_heads(v, c_heads)
                    lf_new = logf[:dec_batch]
                    bias = _suffix_bias(cache_c_logf, o, page_table, lf_new)
                    bias = bias.reshape(dec_batch, page_table.shape[1], 1, page * c_heads)
                    c_o = _decode_attn("fox", sample_heads(q, c_heads), cache_c_k, cache_c_v, o, page_table,
                                       k_h, v_h, n_pg=4, scale=HEAD_DIM ** -0.5, bias=bias)
                    c_out = _pad_rows(c_o.reshape(dec_batch, c_w), SAMPLE_ROWS)
                    outs["cks"].append(k_h[:, None])
                    outs["cvs"].append(v_h[:, None])
                    outs["cfs"].append(lf_new[:, None])
                x = _out_proj([c_out], w_bf["c_w_out"], o, x)
            x = _ffn(x, ffn2_norm[i], w_bf["ffn2_w_gate"], w_bf["ffn2_w_up"], w_bf["ffn2_w_down"], i)
            hp = _rmsnorm(x, ple_norm[i], BF16)
            p2d = p_in[i].reshape(-1, ple_dim)
            if not is_prompt:
                p2d = _pad_rows(p2d, SAMPLE_ROWS)
            grp["x"] = _ple(hp, w_bf["ple_w_gate"], p2d, ple_w_proj, i, x)

    y_prompt = _rmsnorm(groups[0]["x"], final_norm, F32).reshape(batch, seq, d_model)
    y_sample = _rmsnorm(groups[1]["x"], final_norm, F32)[:dec_batch].reshape(dec_batch, dec_seq, d_model)
    st = {k: jnp.stack(v) for k, v in outs.items()}
    return (y_prompt, y_sample,
            st["akp"], st["avp"], st["aks"], st["avs"],
            st["bkp"], st["bvp"], st["bks"], st["bvs"],
            st["ckp"], st["cvp"], st["cfp"], st["cks"], st["cvs"], st["cfs"])
```

```python
import functools
import math
from typing import NamedTuple, Optional

import jax
import jax.numpy as jnp
from jax import lax
from jax.experimental import pallas as pl
from jax.experimental.pallas import tpu as pltpu

F32 = jnp.float32
BF16 = jnp.bfloat16

HEAD_DIM = 128
MOBA_BLOCK = 256
MOBA_TOPK = 3
ROPE_THETA = 10000.0
NORM_EPS = 1e-6
SUBLN_EPS = 1e-5
NEG = -1e30

V7X_VMEM_BYTES = 64 * 1024 * 1024
VMEM_LIMIT = V7X_VMEM_BYTES - 8 * 1024 * 1024
LANES = 128
SAMPLE_ROWS = 16


def _params(*sem):
    return pltpu.CompilerParams(dimension_semantics=sem, vmem_limit_bytes=VMEM_LIMIT)


def _split3(x):
    a = x.astype(BF16)
    r = x - a.astype(F32)
    b = r.astype(BF16)
    c = (r - b.astype(F32)).astype(BF16)
    return a, b, c


def _dot(a, b):
    return jnp.dot(a, b, preferred_element_type=F32)


def _dot_nt(a, b):
    return lax.dot_general(a, b, (((1,), (1,)), ((), ())), preferred_element_type=F32)


class _W(NamedTuple):
    arr: jax.Array
    layer: Optional[int] = None


def _w_spec(w, rows, cols, index):
    if w.layer is None:
        return pl.BlockSpec((rows, cols), index)
    return pl.BlockSpec((1, rows, cols), lambda *g: (w.layer,) + tuple(index(*g)))


def _w_tile(ref):
    tile = ref[...] if len(ref.shape) == 2 else ref[0]
    return tile.astype(BF16)


def _rms(x, g, eps):
    return x * lax.rsqrt(jnp.mean(x * x, axis=-1, keepdims=True) + eps) * g


def _rmsnorm_kernel(x_ref, g_ref, o_ref):
    o_ref[...] = _rms(x_ref[...], g_ref[...], NORM_EPS).astype(o_ref.dtype)


def _rmsnorm(x, g, out_dtype):
    m, d = x.shape
    bm = min(512, m)
    return pl.pallas_call(
        _rmsnorm_kernel,
        grid=(m // bm,),
        in_specs=[pl.BlockSpec((bm, d), lambda i: (i, 0)),
                  pl.BlockSpec((1, d), lambda i: (0, 0))],
        out_specs=pl.BlockSpec((bm, d), lambda i: (i, 0)),
        out_shape=jax.ShapeDtypeStruct((m, d), out_dtype),
        compiler_params=_params("parallel"),
        name="rmsnorm",
    )(x, g.reshape(1, d))


FFN_SLAB = 256


def _ffn_kernel(*refs, emit):
    x_ref, g_ref, wg_ref, wu_ref, wd_ref, o_ref = refs[:6]
    xn_ref = refs[-1]

    @pl.when(pl.program_id(1) == 0)
    def _():
        x = x_ref[...]
        xn_ref[...] = _rms(x, g_ref[...], NORM_EPS).astype(BF16)
        o_ref[...] = x

    tiles = [_w_tile(r) for r in (wg_ref, wu_ref, wd_ref)]
    if emit:
        for copy_ref, tile in zip(refs[6:9], tiles):
            copy_ref[...] = tile
    xn = xn_ref[...]
    gate = _dot(xn, tiles[0])
    up = _dot(xn, tiles[1])
    act = (0.5 * (gate * jax.nn.sigmoid(gate)) * up).astype(BF16)
    o_ref[...] += _dot(act, tiles[2])


def _ffn(x, g, wg, wu, wd):
    m, d = x.shape
    f = wg.arr.shape[-1]
    bm = min(512, m)
    emit = wg.layer is not None
    assert not emit or m == bm
    slab_cols = lambda i, j: (0, j)
    slab_rows = lambda i, j: (j, 0)
    out_specs = [pl.BlockSpec((bm, d), lambda i, j: (i, 0))]
    out_shape = [jax.ShapeDtypeStruct((m, d), F32)]
    if emit:
        out_specs += [pl.BlockSpec((d, FFN_SLAB), slab_cols)] * 2 + [pl.BlockSpec((FFN_SLAB, d), slab_rows)]
        out_shape += [jax.ShapeDtypeStruct((d, f), BF16)] * 2 + [jax.ShapeDtypeStruct((f, d), BF16)]
    res = pl.pallas_call(
        functools.partial(_ffn_kernel, emit=emit),
        grid=(m // bm, f // FFN_SLAB),
        in_specs=[pl.BlockSpec((bm, d), lambda i, j: (i, 0), pipeline_mode=pl.Buffered(1)),
                  pl.BlockSpec((1, d), lambda i, j: (0, 0)),
                  _w_spec(wg, d, FFN_SLAB, slab_cols),
                  _w_spec(wu, d, FFN_SLAB, slab_cols),
                  _w_spec(wd, FFN_SLAB, d, slab_rows)],
        out_specs=out_specs,
        out_shape=out_shape,
        scratch_shapes=[pltpu.VMEM((bm, d), BF16)],
        compiler_params=_params("parallel", "arbitrary"),
        name="ffn",
    )(x, g.reshape(1, d), wg.arr, wu.arr, wd.arr)
    return res[0], (tuple(res[1:]) if emit else None)


def _rope_tables(pos, kind):
    width = HEAD_DIM if kind == "a" else HEAD_DIM // 2
    half = width // 2
    inv_freq = jnp.power(ROPE_THETA, -jnp.arange(half, dtype=F32) / half)
    ang = pos.astype(F32)[:, None] * inv_freq[None, :]
    cos, sin = jnp.cos(ang), jnp.sin(ang)
    reps = HEAD_DIM // width
    cos_t = jnp.tile(jnp.concatenate([cos, cos], axis=-1), (1, reps))
    sin_t = jnp.tile(jnp.concatenate([-sin, sin], axis=-1), (1, reps))
    return cos_t, sin_t


def _rope_head(x, cos, sin, kind):
    if kind == "a":
        partner = pltpu.roll(x, HEAD_DIM // 2, 1)
    else:
        lane = lax.broadcasted_iota(jnp.int32, x.shape, 1)
        first = (lane & (HEAD_DIM // 2 - 1)) < (HEAD_DIM // 4)
        partner = jnp.where(first, pltpu.roll(x, HEAD_DIM - HEAD_DIM // 4, 1), pltpu.roll(x, HEAD_DIM // 4, 1))
    return x * cos + partner * sin


def _proj_kernel(*refs, rope, n_out, emit):
    if rope is None:
        a_ref, w_ref = refs[:2]
        outs = refs[2:]
    else:
        a_ref, w_ref, cos_ref, sin_ref = refs[:4]
        outs = refs[4:]
    tile = _w_tile(w_ref)
    if emit:
        outs[n_out][...] = tile
    acc = _dot(a_ref[...], tile)
    if rope is None:
        for o in outs[:n_out]:
            o[...] = acc.astype(o.dtype)
        return
    cos, sin = cos_ref[...], sin_ref[...]
    for h in range(acc.shape[1] // HEAD_DIM):
        cols = slice(h * HEAD_DIM, (h + 1) * HEAD_DIM)
        head = _rope_head(acc[:, cols], cos, sin, rope)
        for o in outs[:n_out]:
            o[:, cols] = head.astype(o.dtype)


def _proj(a, w, col0, width, out_dtypes, rope=None, tables=None):
    m, k = a.shape
    bm = min(1024, m) if rope is None else min(1024, m, tables[0].shape[0])
    bn = min(512, width)
    emit = w.layer is not None
    assert col0 % bn == 0 and width % bn == 0 and (not emit or m == bm)
    first = col0 // bn if emit else 0
    in_specs = [pl.BlockSpec((bm, k), lambda i, j: (i, 0)),
                _w_spec(w, k, bn, lambda i, j: (0, first + j))]
    args = [a, w.arr]
    if rope is not None:
        nt = tables[0].shape[0] // bm
        in_specs += [pl.BlockSpec((bm, HEAD_DIM), lambda i, j: (i % nt, 0))] * 2
        args += list(tables)
    out_specs = [pl.BlockSpec((bm, bn), lambda i, j: (i, j))] * len(out_dtypes)
    out_shape = [jax.ShapeDtypeStruct((m, width), dt) for dt in out_dtypes]
    if emit:
        out_specs.append(pl.BlockSpec((k, bn), lambda i, j: (0, j)))
        out_shape.append(jax.ShapeDtypeStruct((k, width), BF16))
    outs = pl.pallas_call(
        functools.partial(_proj_kernel, rope=rope, n_out=len(out_dtypes), emit=emit),
        grid=(m // bm, width // bn),
        in_specs=in_specs,
        out_specs=out_specs,
        out_shape=out_shape,
        compiler_params=_params("parallel", "parallel"),
        name="proj_" + (rope or "plain"),
    )(*args)
    return (outs[:-1], outs[-1]) if emit else (outs, None)


def _logf_kernel(a_ref, w_ref, b_ref, o_ref):
    z = _dot(a_ref[...], w_ref[...].astype(BF16)) + b_ref[...]
    o_ref[...] = jnp.minimum(z, 0.0) - jnp.log1p(jnp.exp(-jnp.abs(z)))


def _logf_proj(a, w_f, b_f):
    m, k = a.shape
    bm = min(1024, m)
    return pl.pallas_call(
        _logf_kernel,
        grid=(m // bm,),
        in_specs=[pl.BlockSpec((bm, k), lambda i: (i, 0)),
                  pl.BlockSpec((k, LANES), lambda i: (0, 0)),
                  pl.BlockSpec((1, LANES), lambda i: (0, 0))],
        out_specs=pl.BlockSpec((bm, LANES), lambda i: (i, 0)),
        out_shape=jax.ShapeDtypeStruct((m, LANES), F32),
        compiler_params=_params("parallel"),
        name="logf_proj",
    )(a, w_f, b_f)


def _out_proj_kernel(*refs, n_lhs, emit):
    a_refs = refs[:n_lhs]
    w_ref, x_ref, o_ref = refs[n_lhs:n_lhs + 3]
    tile = _w_tile(w_ref)
    if emit:
        refs[n_lhs + 3][...] = tile
    acc = x_ref[...]
    k0 = 0
    for a_ref in a_refs:
        kk = a_ref.shape[1]
        acc = acc + _dot(a_ref[...], tile[k0:k0 + kk, :])
        k0 += kk
    o_ref[...] = acc


def _tiled_call(kernel, name, m, n, in_specs, args, emit, k):
    bm, bn = min(1024, m), min(512, n)
    assert not emit or m == bm
    out_specs = [pl.BlockSpec((bm, bn), lambda i, j: (i, j))]
    out_shape = [jax.ShapeDtypeStruct((m, n), F32)]
    if emit:
        out_specs.append(pl.BlockSpec((k, bn), lambda i, j: (0, j)))
        out_shape.append(jax.ShapeDtypeStruct((k, n), BF16))
    res = pl.pallas_call(
        kernel,
        grid=(m // bm, n // bn),
        in_specs=in_specs(bm, bn),
        out_specs=out_specs,
        out_shape=out_shape,
        compiler_params=_params("parallel", "parallel"),
        name=name,
    )(*args)
    return res[0], (res[1] if emit else None)


def _out_proj(lhs, w, x):
    m, n = x.shape
    k = w.arr.shape[-2]
    emit = w.layer is not None

    def in_specs(bm, bn):
        return ([pl.BlockSpec((bm, a.shape[1]), lambda i, j: (i, 0)) for a in lhs]
                + [_w_spec(w, k, bn, lambda i, j: (0, j)), pl.BlockSpec((bm, bn), lambda i, j: (i, j))])

    return _tiled_call(functools.partial(_out_proj_kernel, n_lhs=len(lhs), emit=emit), "out_proj", m, n,
                       in_specs, [*lhs, w.arr, x], emit, k)


def _ple_kernel(*refs, emit):
    a_ref, wg_ref, p_ref, wp_ref, x_ref, o_ref = refs[:6]
    tile = _w_tile(wg_ref)
    if emit:
        refs[6][...] = tile
    gate = jax.nn.sigmoid(_dot(a_ref[...], tile))
    emb = _dot(p_ref[...].astype(BF16), wp_ref[0].astype(BF16))
    o_ref[...] = x_ref[...] + gate * emb


def _ple(a, wg, p, wp, layer, x):
    m, n = x.shape
    k = a.shape[1]
    kp = p.shape[1]
    emit = wg.layer is not None

    def in_specs(bm, bn):
        return [pl.BlockSpec((bm, k), lambda i, j: (i, 0)),
                _w_spec(wg, k, bn, lambda i, j: (0, j)),
                pl.BlockSpec((bm, kp), lambda i, j: (i, 0)),
                pl.BlockSpec((1, kp, bn), lambda i, j: (layer, 0, j)),
                pl.BlockSpec((bm, bn), lambda i, j: (i, j))]

    return _tiled_call(functools.partial(_ple_kernel, emit=emit), "ple", m, n, in_specs, [a, wg.arr, p, wp, x],
                       emit, k)


ATTN_TQ = 256


def _diag_mask():
    r = lax.broadcasted_iota(jnp.int32, (ATTN_TQ, ATTN_TQ), 0)
    c = lax.broadcasted_iota(jnp.int32, (ATTN_TQ, ATTN_TQ), 1)
    return c <= r


def _causal_softmax(t_raw, i, scale):
    kprev = i * ATTN_TQ
    c = scale * math.log2(math.e)
    t_d = jnp.where(_diag_mask(), t_raw[:, kprev:], NEG)
    m = jnp.max(t_d, axis=-1, keepdims=True)
    if i:
        t_p = t_raw[:, :kprev]
        m = jnp.maximum(m, jnp.max(t_p, axis=-1, keepdims=True))
    e_d = jnp.exp2((t_d - m) * c)
    tot = jnp.sum(e_d, axis=-1, keepdims=True)
    if i:
        e_p = jnp.exp2((t_p - m) * c)
        tot = tot + jnp.sum(e_p, axis=-1, keepdims=True)
    inv = 1.0 / tot
    return (e_p * inv if i else None), e_d * inv


def _pv(p_prev, p_diag, v_ref, i):
    kprev = i * ATTN_TQ
    o = _dot(p_diag.astype(BF16), v_ref[kprev:kprev + ATTN_TQ, :].astype(BF16))
    if i:
        o = o + _dot(p_prev.astype(BF16), v_ref[0:kprev, :].astype(BF16))
    return o


def _moba_prompt_kernel(q_ref, k_ref, v_ref, o_ref):
    t = q_ref.shape[0]
    nb = t // MOBA_BLOCK
    assert ATTN_TQ == MOBA_BLOCK and nb <= LANES
    scale = HEAD_DIM ** -0.5
    km = jnp.sum(k_ref[...].reshape(nb, MOBA_BLOCK, HEAD_DIM), axis=1) * (1.0 / MOBA_BLOCK)
    km = jnp.concatenate([km, jnp.zeros((LANES - nb, HEAD_DIM), F32)], axis=0)
    km_hi = km.astype(BF16)
    km_lo = (km - km_hi.astype(F32)).astype(BF16)
    key_blk = lax.broadcasted_iota(jnp.int32, (t, LANES), 0) >> (MOBA_BLOCK.bit_length() - 1)
    onehot = (key_blk == lax.broadcasted_iota(jnp.int32, (t, LANES), 1)).astype(F32).astype(BF16)
    k_aug = jnp.concatenate([k_ref[...].astype(BF16), onehot], axis=1)
    for i in range(nb):
        rows = slice(i * ATTN_TQ, (i + 1) * ATTN_TQ)
        kend = (i + 1) * MOBA_BLOCK
        q = q_ref[rows, :]
        if i > MOBA_TOPK:
            gate = _dot_nt(q, km_hi) + _dot_nt(q, km_lo)
            lane = lax.broadcasted_iota(jnp.int32, gate.shape, 1)
            rank = jnp.zeros(gate.shape, F32)
            for m in range(i):
                g_m = gate[:, m:m + 1]
                ahead = (g_m > gate) | ((g_m == gate) & (lane > m))
                rank = rank + ahead.astype(F32)
            bias = jnp.where((rank < MOBA_TOPK) | (lane >= i), 0.0, NEG)
            q_aug = jnp.concatenate([q, bias.astype(BF16)], axis=1)
            t_raw = _dot_nt(q_aug, k_aug[0:kend, :])
        else:
            t_raw = _dot_nt(q, k_aug[0:kend, 0:HEAD_DIM])
        p_prev, p_diag = _causal_softmax(t_raw, i, scale)
        o_ref[rows, :] = _pv(p_prev, p_diag, v_ref, i).astype(o_ref.dtype)


def _diff_lambda(lq1, lk1, lq2, lk2, lam_init):
    return (jnp.exp(jnp.sum(lq1 * lk1, axis=-1, keepdims=True))
            - jnp.exp(jnp.sum(lq2 * lk2, axis=-1, keepdims=True)) + lam_init)


def _diff_finish(o, subln, lam_init):
    return _rms(o, subln, SUBLN_EPS) * (1.0 - lam_init)


def _diff_prompt_kernel(q_ref, k_ref, v_ref, lq1_ref, lk1_ref, lq2_ref, lk2_ref, g_ref, o_ref, *, lam_init):
    t = q_ref.shape[0]
    half = HEAD_DIM // 2
    scale = half ** -0.5
    lam = _diff_lambda(lq1_ref[...], lk1_ref[...], lq2_ref[...], lk2_ref[...], lam_init)
    lane = lax.broadcasted_iota(jnp.int32, (ATTN_TQ, HEAD_DIM), 1)
    for i in range(t // ATTN_TQ):
        rows = slice(i * ATTN_TQ, (i + 1) * ATTN_TQ)
        kend = (i + 1) * ATTN_TQ
        q = q_ref[rows, :]
        k = k_ref[0:kend, :].astype(BF16)
        p1_prev, p1_diag = _causal_softmax(_dot_nt(jnp.where(lane < half, q, jnp.zeros_like(q)), k), i, scale)
        p2_prev, p2_diag = _causal_softmax(_dot_nt(jnp.where(lane >= half, q, jnp.zeros_like(q)), k), i, scale)
        o = _pv((p1_prev - lam * p2_prev) if i else None, p1_diag - lam * p2_diag, v_ref, i)
        o_ref[rows, :] = _diff_finish(o, g_ref[...], lam_init).astype(o_ref.dtype)


def _fox_prompt_kernel(q_ref, k_ref, v_ref, ccol_ref, o_ref):
    t = q_ref.shape[0]
    scale = HEAD_DIM ** -0.5
    hi, mid, lo = (term.astype(F32) for term in _split3(ccol_ref[0, 0] * (1.0 / scale)))
    lane = lax.broadcasted_iota(jnp.int32, (t, LANES), 1)
    one = jnp.ones((t, LANES), F32)

    def features(vals):
        out = jnp.zeros((t, LANES), F32)
        for n, val in enumerate(vals):
            out = jnp.where(lane == n, val, out)
        return out.astype(BF16)

    q_feat = features([hi, mid, lo, one, one, one])
    k_aug = jnp.concatenate([k_ref[...].astype(BF16), features([one, one, one, -hi, -mid, -lo])], axis=1)
    for i in range(t // ATTN_TQ):
        rows = slice(i * ATTN_TQ, (i + 1) * ATTN_TQ)
        kend = (i + 1) * ATTN_TQ
        q_aug = jnp.concatenate([q_ref[rows, :], q_feat[rows, :]], axis=1)
        p_prev, p_diag = _causal_softmax(_dot_nt(q_aug, k_aug[0:kend, :]), i, scale)
        o_ref[rows, :] = _pv(p_prev, p_diag, v_ref, i).astype(o_ref.dtype)


def _head_spec(t):
    return pl.BlockSpec((t, HEAD_DIM), lambda b, h: (b, h))


def _vec_spec(n):
    return pl.BlockSpec((1, n), lambda b, h: (0, 0))


def _prompt_attn(kernel, name, batch, t, heads, args, in_specs):
    return pl.pallas_call(
        kernel,
        grid=(batch, heads),
        in_specs=in_specs,
        out_specs=_head_spec(t),
        out_shape=jax.ShapeDtypeStruct((batch * t, heads * HEAD_DIM), BF16),
        compiler_params=_params("parallel", "parallel"),
        name=name,
    )(*args)


def _tri_dot(tri, x):
    a, b, c = _split3(x)
    return _dot(tri, a) + _dot(tri, b) + _dot(tri, c)


CUMSUM_CHUNK = 256


def _cumsum_kernel(lf_ref, o_ref):
    t = lf_ref.shape[1]
    r = lax.broadcasted_iota(jnp.int32, (CUMSUM_CHUNK, CUMSUM_CHUNK), 0)
    c = lax.broadcasted_iota(jnp.int32, (CUMSUM_CHUNK, CUMSUM_CHUNK), 1)
    tri = (c <= r).astype(BF16)
    carry = jnp.zeros((1, lf_ref.shape[2]), F32)
    for j in range(t // CUMSUM_CHUNK):
        rows = slice(j * CUMSUM_CHUNK, (j + 1) * CUMSUM_CHUNK)
        cs = _tri_dot(tri, lf_ref[0, rows, :]) + carry
        o_ref[0, rows, :] = cs
        carry = cs[CUMSUM_CHUNK - 1:CUMSUM_CHUNK, :]


def _cumsum_time(lf):
    b, t, h = lf.shape
    return pl.pallas_call(
        _cumsum_kernel,
        grid=(b,),
        in_specs=[pl.BlockSpec((1, t, h), lambda i: (i, 0, 0))],
        out_specs=pl.BlockSpec((1, t, h), lambda i: (i, 0, 0)),
        out_shape=jax.ShapeDtypeStruct(lf.shape, F32),
        compiler_params=_params("parallel"),
        name="cumsum_time",
    )(lf)


SUFFIX_PAGES = 8


def _suffix_kernel(*refs, n_pg):
    lf_refs = refs[1:1 + n_pg]
    new_ref, o_ref, carry_ref = refs[1 + n_pg:]

    @pl.when(pl.program_id(1) == 0)
    def _():
        carry_ref[...] = new_ref[0]

    page = lf_refs[0].shape[2]
    r = lax.broadcasted_iota(jnp.int32, (page, page), 0)
    c = lax.broadcasted_iota(jnp.int32, (page, page), 1)
    tri = (c > r).astype(BF16)
    carry = carry_ref[...]
    for g in reversed(range(n_pg)):
        lf = lf_refs[g][0, 0]
        o_ref[0, g] = _tri_dot(tri, lf) + carry
        carry = carry + jnp.sum(lf, axis=0, keepdims=True)
    carry_ref[...] = carry


def _suffix_bias(cache_lf, layer, page_table, lf_new):
    bsz, n_pages = page_table.shape
    page, h = cache_lf.shape[2], cache_lf.shape[3]
    n_pg = math.gcd(SUFFIX_PAGES, n_pages)
    steps = n_pages // n_pg

    def spec(g):
        return pl.BlockSpec((1, 1, page, h), lambda b, j, pt: (layer, pt[b, (steps - 1 - j) * n_pg + g], 0, 0))

    grid_spec = pltpu.PrefetchScalarGridSpec(
        num_scalar_prefetch=1,
        grid=(bsz, steps),
        in_specs=[spec(g) for g in range(n_pg)] + [pl.BlockSpec((1, 1, h), lambda b, j, pt: (b, 0, 0))],
        out_specs=pl.BlockSpec((1, n_pg, page, h), lambda b, j, pt: (b, steps - 1 - j, 0, 0)),
        scratch_shapes=[pltpu.VMEM((1, h), F32)],
    )
    return pl.pallas_call(
        functools.partial(_suffix_kernel, n_pg=n_pg),
        grid_spec=grid_spec,
        out_shape=jax.ShapeDtypeStruct((bsz, n_pages, page, h), F32),
        compiler_params=_params("parallel", "arbitrary"),
        name="suffix_bias",
    )(page_table, *([cache_lf] * n_pg), lf_new.reshape(bsz, 1, h))


KMEAN_BLOCKS = 4


def _kmean_kernel(*refs, per, blocks):
    k_refs, o_ref = refs[1:-1], refs[-1]
    for n in range(blocks):
        tot = jnp.sum(k_refs[per * n][0, 0], axis=0)
        for g in range(1, per):
            tot = tot + jnp.sum(k_refs[per * n + g][0, 0], axis=0)
        o_ref[0, n] = tot * (1.0 / MOBA_BLOCK)


def _kmean_pages(cache_k, layer, page_table):
    bsz, n_pages = page_table.shape
    page, h, d = cache_k.shape[2:]
    per = MOBA_BLOCK // page
    nblk = n_pages // per
    blocks = math.gcd(KMEAN_BLOCKS, nblk)
    n_pg = per * blocks

    def spec(g):
        return pl.BlockSpec((1, 1, page, h, d), lambda b, n, pt: (layer, pt[b, n_pg * n + g], 0, 0, 0))

    grid_spec = pltpu.PrefetchScalarGridSpec(
        num_scalar_prefetch=1,
        grid=(bsz, nblk // blocks),
        in_specs=[spec(g) for g in range(n_pg)],
        out_specs=pl.BlockSpec((1, blocks, h, d), lambda b, n, pt: (b, n, 0, 0)),
    )
    return pl.pallas_call(
        functools.partial(_kmean_kernel, per=per, blocks=blocks),
        grid_spec=grid_spec,
        out_shape=jax.ShapeDtypeStruct((bsz, nblk, h, d), F32),
        compiler_params=_params("parallel", "parallel"),
        name="kmean_pages",
    )(page_table, *([cache_k] * n_pg))


def _moba_select_kernel(km_ref, q_ref, o_ref):
    km = km_ref[0]
    gate = jnp.sum(km * q_ref[...], axis=-1, keepdims=True)
    nblk = gate.shape[0]
    idx = lax.broadcasted_iota(jnp.int32, gate.shape, 0).astype(F32)
    for s in range(MOBA_TOPK):
        best = jnp.max(gate, axis=0, keepdims=True)
        first = jnp.min(jnp.where(gate == best, idx, float(nblk)), axis=0, keepdims=True)
        o_ref[0, s] = first[0]
        gate = jnp.where(idx == first, -jnp.inf, gate)


def _moba_select(kmean, q):
    bsz, nblk, h, d = kmean.shape
    assert nblk >= MOBA_TOPK
    picks = pl.pallas_call(
        _moba_select_kernel,
        grid=(bsz,),
        in_specs=[pl.BlockSpec((1, nblk, h, d), lambda b: (b, 0, 0, 0)),
                  pl.BlockSpec((1, h, d), lambda b: (b, 0, 0))],
        out_specs=pl.BlockSpec((1, MOBA_TOPK, h, 1), lambda b: (b, 0, 0, 0)),
        out_shape=jax.ShapeDtypeStruct((bsz, MOBA_TOPK, h, 1), F32),
        compiler_params=_params("parallel"),
        name="moba_select",
    )(kmean, q)
    return picks[..., 0].astype(jnp.int32)


def _moba_decode_kernel(pt_ref, sel_ref, q_ref, k0_ref, k1_ref, v0_ref, v1_ref, knew_ref, vnew_ref, o_ref,
                        m_ref, l_ref, acc_ref, *, scale):
    s_id = pl.program_id(2)

    @pl.when(s_id == 0)
    def _():
        m_ref[...] = jnp.full(m_ref.shape, NEG, F32)
        l_ref[...] = jnp.zeros(l_ref.shape, F32)
        acc_ref[...] = jnp.zeros(acc_ref.shape, F32)

    q = q_ref[0, 0]
    page, grp = k0_ref.shape[2], k0_ref.shape[3]
    width = page * grp
    lane = lax.broadcasted_iota(jnp.int32, (1, width), 1)
    mine = (lane & (grp - 1)) == (pl.program_id(1) & (grp - 1))

    def update(scores, oks, pvs):
        m_old = m_ref[...]
        m_new = m_old
        for s in scores:
            m_new = jnp.maximum(m_new, jnp.max(s, axis=-1, keepdims=True))
        alpha = jnp.exp(m_old - m_new)
        l = alpha * l_ref[...]
        acc = alpha * acc_ref[...]
        for s, ok, pv in zip(scores, oks, pvs):
            p = jnp.exp(s - m_new)
            if ok is not None:
                p = jnp.where(ok, p, 0.0)
            l = l + jnp.sum(p, axis=-1, keepdims=True)
            acc = acc + pv(p.astype(BF16))
        l_ref[...] = l
        acc_ref[...] = acc
        m_ref[...] = m_new

    pages = ((k0_ref, v0_ref), (k1_ref, v1_ref))
    update([jnp.where(mine, _dot_nt(q, k_ref[0, 0].reshape(width, HEAD_DIM).astype(BF16)) * scale, NEG)
            for k_ref, _ in pages],
           [mine] * len(pages),
           [lambda p, v_ref=v_ref: _dot(p, v_ref[0, 0].reshape(width, HEAD_DIM).astype(BF16)) for _, v_ref in pages])

    @pl.when(s_id == pl.num_programs(2) - 1)
    def _():
        knew = knew_ref[0, 0].astype(BF16).astype(F32)
        vnew = vnew_ref[0, 0].astype(BF16).astype(F32)
        update([jnp.sum(q.astype(F32) * knew, axis=-1, keepdims=True) * scale], [None],
               [lambda p: p.astype(F32) * vnew])
        o_ref[0, 0] = (acc_ref[...] / l_ref[...]).astype(o_ref.dtype)


MOBA_HEAD_GROUP = 8


def _moba_decode(q, cache_k, cache_v, layer, page_table, picks, k_new, v_new):
    bsz, heads, d = q.shape
    page = cache_k.shape[2]
    per = MOBA_BLOCK // page
    grp = min(MOBA_HEAD_GROUP, heads)
    assert per == 2 and heads % grp == 0

    def page_spec(g):
        return pl.BlockSpec((1, 1, page, grp, d),
                            lambda b, h, s, pt, sel: (layer, pt[b, per * sel[b, s, h] + g], 0, h // grp, 0))

    row = pl.BlockSpec((1, 1, 1, d), lambda b, h, s, pt, sel: (b, h, 0, 0))
    grid_spec = pltpu.PrefetchScalarGridSpec(
        num_scalar_prefetch=2,
        grid=(bsz, heads, picks.shape[1]),
        in_specs=[row, page_spec(0), page_spec(1), page_spec(0), page_spec(1), row, row],
        out_specs=row,
        scratch_shapes=[pltpu.VMEM((1, 1), F32), pltpu.VMEM((1, 1), F32), pltpu.VMEM((1, d), F32)],
    )
    out = pl.pallas_call(
        functools.partial(_moba_decode_kernel, scale=HEAD_DIM ** -0.5),
        grid_spec=grid_spec,
        out_shape=jax.ShapeDtypeStruct((bsz, heads, 1, d), BF16),
        compiler_params=_params("parallel", "parallel", "arbitrary"),
        name="decode_moba",
    )(page_table, picks, q.reshape(bsz, heads, 1, d), cache_k, cache_k, cache_v, cache_v,
      k_new.reshape(bsz, heads, 1, d), v_new.reshape(bsz, heads, 1, d))
    return out.reshape(bsz, heads, d)


def _decode_kernel(*refs, mode, n_pg, heads, scale, lam_init):
    pt_ref, q_ref = refs[0], refs[1]
    k_refs = refs[2:2 + n_pg]
    v_refs = refs[2 + n_pg:2 + 2 * n_pg]
    pos = 2 + 2 * n_pg
    if mode == "fox":
        bias_refs = refs[pos:pos + n_pg]
        pos += n_pg
    knew_ref, vnew_ref = refs[pos], refs[pos + 1]
    pos += 2
    if mode == "diff":
        lq1_ref, lk1_ref, lq2_ref, lk2_ref, g_ref = refs[pos:pos + 5]
        pos += 5
    o_ref, m_ref, l_ref, acc_ref = refs[pos:pos + 4]

    j = pl.program_id(1)

    @pl.when(j == 0)
    def _():
        m_ref[...] = jnp.full(m_ref.shape, NEG, F32)
        l_ref[...] = jnp.zeros(l_ref.shape, F32)
        acc_ref[...] = jnp.zeros(acc_ref.shape, F32)

    q = q_ref[0]
    nrow = q.shape[0]
    page = k_refs[0].shape[2]
    width = page * heads
    lane = lax.broadcasted_iota(jnp.int32, (nrow, width), 1)
    row = lax.broadcasted_iota(jnp.int32, (nrow, width), 0)
    assert heads & (heads - 1) == 0
    ok = (lane & (heads - 1)) == (row & (heads - 1))

    scores = []
    for g in range(n_pg):
        kf = k_refs[g][0, 0].reshape(width, HEAD_DIM).astype(BF16)
        s = _dot_nt(q, kf) * scale
        if mode == "fox":
            s = s + bias_refs[g][0, 0]
        scores.append(jnp.where(ok, s, NEG))
    m_old = m_ref[...]
    m_new = m_old
    for s in scores:
        m_new = jnp.maximum(m_new, jnp.max(s, axis=-1, keepdims=True))
    alpha = jnp.exp(m_old - m_new)
    l = alpha * l_ref[...]
    acc = alpha * acc_ref[...]
    for g, s in enumerate(scores):
        p = jnp.where(ok, jnp.exp(s - m_new), 0.0)
        l = l + jnp.sum(p, axis=-1, keepdims=True)
        acc = acc + _dot(p.astype(BF16), v_refs[g][0, 0].reshape(width, HEAD_DIM).astype(BF16))
    m_ref[...] = m_new
    l_ref[...] = l
    acc_ref[...] = acc

    @pl.when(j == pl.num_programs(1) - 1)
    def _():
        reps = nrow // heads
        knew = knew_ref[0].astype(BF16).astype(F32)
        vnew = vnew_ref[0].astype(BF16).astype(F32)
        if reps > 1:
            knew = jnp.concatenate([knew] * reps, axis=0)
            vnew = jnp.concatenate([vnew] * reps, axis=0)
        s_new = jnp.sum(q.astype(F32) * knew, axis=-1, keepdims=True) * scale
        m_old = m_ref[...]
        m_new = jnp.maximum(m_old, s_new)
        alpha = jnp.exp(m_old - m_new)
        p_new = jnp.exp(s_new - m_new)
        l = alpha * l_ref[...] + p_new
        acc = alpha * acc_ref[...] + p_new.astype(BF16).astype(F32) * vnew
        o = acc / l
        if mode == "diff":
            lam = _diff_lambda(lq1_ref[...], lk1_ref[...], lq2_ref[...], lk2_ref[...], lam_init)
            o = _diff_finish(o[0:heads] - lam * o[heads:2 * heads], g_ref[...], lam_init)
        o_ref[0] = o.astype(o_ref.dtype)


def _decode_attn(mode, q, cache_k, cache_v, layer, page_table, k_new, v_new, *, n_pg, scale,
                 bias=None, diff_params=None, lam_init=0.0):
    bsz, n_pages = page_table.shape
    page, heads, d = cache_k.shape[2:]
    nrow = q.shape[1]
    n_pg = math.gcd(n_pg, n_pages)
    steps = n_pages // n_pg

    def page_spec(g):
        return pl.BlockSpec((1, 1, page, heads, d), lambda b, j, pt: (layer, pt[b, n_pg * j + g], 0, 0, 0))

    in_specs = [pl.BlockSpec((1, nrow, d), lambda b, j, pt: (b, 0, 0))]
    in_specs += [page_spec(g) for g in range(n_pg)] * 2
    args = [q] + [cache_k] * n_pg + [cache_v] * n_pg
    if mode == "fox":
        for g in range(n_pg):
            in_specs.append(pl.BlockSpec((1, 1, 1, page * heads),
                                         functools.partial(lambda b, j, pt, g: (b, n_pg * j + g, 0, 0), g=g)))
        args += [bias] * n_pg
    in_specs += [pl.BlockSpec((1, heads, d), lambda b, j, pt: (b, 0, 0))] * 2
    args += [k_new, v_new]
    if mode == "diff":
        for p in diff_params:
            in_specs.append(pl.BlockSpec((1, p.shape[1]), lambda b, j, pt: (0, 0)))
        args += list(diff_params)
    grid_spec = pltpu.PrefetchScalarGridSpec(
        num_scalar_prefetch=1,
        grid=(bsz, steps),
        in_specs=in_specs,
        out_specs=pl.BlockSpec((1, heads, d), lambda b, j, pt: (b, 0, 0)),
        scratch_shapes=[pltpu.VMEM((nrow, 1), F32), pltpu.VMEM((nrow, 1), F32), pltpu.VMEM((nrow, d), F32)],
    )
    return pl.pallas_call(
        functools.partial(_decode_kernel, mode=mode, n_pg=n_pg, heads=heads, scale=scale, lam_init=lam_init),
        grid_spec=grid_spec,
        out_shape=jax.ShapeDtypeStruct((bsz, heads, d), BF16),
        compiler_params=_params("parallel", "arbitrary"),
        name="decode_" + mode,
    )(page_table, *args)


def _pad_rows(x, rows):
    return jnp.pad(x, ((0, rows - x.shape[0]), (0, 0)))


def _heads5(x2d, batch, t, heads):
    return x2d.reshape(1, batch, t, heads, HEAD_DIM)


def kernel(x_prompt, x_sample, cache_a_k, cache_a_v, cache_b_k, cache_b_v, cache_c_k, cache_c_v, cache_c_logf, page_table, p_prompt, p_sample, ffn1_norm, ffn1_w_gate, ffn1_w_up, ffn1_w_down, mix_norm, ab_w_in, b_lambda_q1, b_lambda_k1, b_lambda_q2, b_lambda_k2, b_subln, ab_w_out, c_w_in, c_b_f, c_w_out, ffn2_norm, ffn2_w_gate, ffn2_w_up, ffn2_w_down, ple_norm, ple_w_gate, ple_w_proj, final_norm):
    batch, seq, d_model = x_prompt.shape
    dec_batch, dec_seq, _ = x_sample.shape
    assert dec_seq == 1 and dec_batch <= SAMPLE_ROWS
    depth = ffn1_norm.shape[0]
    page = cache_a_k.shape[2]
    past_len = page_table.shape[1] * page
    a_heads, b_heads, c_heads = cache_a_k.shape[3], cache_b_k.shape[3], cache_c_k.shape[3]
    a_w, b_w, c_w = a_heads * HEAD_DIM, b_heads * HEAD_DIM, c_heads * HEAD_DIM
    m_p = batch * seq
    ple_dim = p_prompt.shape[-1]

    pos_p = jnp.arange(seq, dtype=jnp.int32)
    pos_s = jnp.full((SAMPLE_ROWS,), past_len, jnp.int32)
    rope_p = {k: _rope_tables(pos_p, k) for k in "ab"}
    rope_s = {k: _rope_tables(pos_s, k) for k in "ab"}

    copies = {}

    def weight(name, param, layer, col0=0):
        return copies.get((name, layer, col0), _W(param, layer))

    def ffn(x, norm, named_params, layer):
        out, made = _ffn(x, norm, *[weight(n, p, layer) for n, p in named_params])
        if made is not None:
            for (n, _), c in zip(named_params, made):
                copies[(n, layer, 0)] = _W(c)
        return out

    def proj(h, name, param, layer, col0, width, dtypes, rope_kind=None, tables=None):
        res, made = _proj(h, weight(name, param, layer, col0), col0, width, dtypes, rope_kind, tables)
        if made is not None:
            copies[(name, layer, col0)] = _W(made)
        return res

    def out_proj(lhs, name, param, layer, x):
        res, made = _out_proj(lhs, weight(name, param, layer), x)
        if made is not None:
            copies[(name, layer, 0)] = _W(made)
        return res

    xp = x_prompt.reshape(m_p, d_model)
    xs = _pad_rows(x_sample.reshape(dec_batch, d_model), SAMPLE_ROWS)
    final = {}
    outs = {k: [] for k in ("akp", "avp", "aks", "avs", "bkp", "bvp", "bks", "bvs",
                            "ckp", "cvp", "cfp", "cks", "cvs", "cfs")}

    def sample_heads(x2d, heads):
        return x2d[:dec_batch].reshape(dec_batch, heads, HEAD_DIM)

    for is_prompt, x, p_in in ((False, xs, p_sample), (True, xp, p_prompt)):
        for i in range(depth):
            x = ffn(x, ffn1_norm[i], [("ffn1_w_gate", ffn1_w_gate), ("ffn1_w_up", ffn1_w_up),
                                      ("ffn1_w_down", ffn1_w_down)], i)
            h = _rmsnorm(x, mix_norm[i], BF16)
            rope = rope_p if is_prompt else rope_s
            if i % 2 == 0:
                e = i // 2
                lam_init = 0.8 - 0.6 * math.exp(-0.3 * i)
                (qa,) = proj(h, "ab_w_in", ab_w_in, e, 0, a_w, [BF16], "a", rope["a"])
                (ka,) = proj(h, "ab_w_in", ab_w_in, e, a_w, a_w, [F32], "a", rope["a"])
                (va,) = proj(h, "ab_w_in", ab_w_in, e, 2 * a_w, a_w, [F32])
                (qb,) = proj(h, "ab_w_in", ab_w_in, e, 3 * a_w, b_w, [BF16], "b", rope["b"])
                (kb,) = proj(h, "ab_w_in", ab_w_in, e, 3 * a_w + b_w, b_w, [F32], "b", rope["b"])
                (vb,) = proj(h, "ab_w_in", ab_w_in, e, 3 * a_w + 2 * b_w, b_w, [F32])
                diff_params = [v[e].reshape(1, -1) for v in (b_lambda_q1, b_lambda_k1, b_lambda_q2, b_lambda_k2,
                                                             b_subln)]
                if is_prompt:
                    hs = _head_spec(seq)
                    a_out = _prompt_attn(_moba_prompt_kernel, "moba_prompt", batch, seq, a_heads,
                                         [qa, ka, va], [hs] * 3)
                    b_out = _prompt_attn(functools.partial(_diff_prompt_kernel, lam_init=lam_init), "diff_prompt",
                                         batch, seq, b_heads, [qb, kb, vb] + diff_params,
                                         [hs] * 3 + [_vec_spec(p.shape[1]) for p in diff_params])
                    for key, val, nh in (("akp", ka, a_heads), ("avp", va, a_heads),
                                         ("bkp", kb, b_heads), ("bvp", vb, b_heads)):
                        outs[key].append(val.reshape(batch, seq, nh, HEAD_DIM))
                else:
                    ka_h, va_h = sample_heads(ka, a_heads), sample_heads(va, a_heads)
                    kb_h, vb_h = sample_heads(kb, b_heads), sample_heads(vb, b_heads)
                    (qa_f32,) = proj(h, "ab_w_in", ab_w_in, e, 0, a_w, [F32], "a", rope["a"])
                    kmean = _kmean_pages(cache_a_k, e, page_table)
                    picks = _moba_select(kmean, sample_heads(qa_f32, a_heads))
                    a_o = _moba_decode(sample_heads(qa, a_heads), cache_a_k, cache_a_v, e, page_table, picks,
                                       ka_h, va_h)
                    qb_h = sample_heads(qb, b_heads)
                    lane = jnp.arange(HEAD_DIM) < HEAD_DIM // 2
                    qb2 = jnp.concatenate([jnp.where(lane, qb_h, 0), jnp.where(lane, 0, qb_h)], axis=1)
                    b_o = _decode_attn("diff", qb2, cache_b_k, cache_b_v, e, page_table, kb_h, vb_h,
                                       n_pg=8, scale=(HEAD_DIM // 2) ** -0.5, diff_params=diff_params,
                                       lam_init=lam_init)
                    a_out = _pad_rows(a_o.reshape(dec_batch, a_w), SAMPLE_ROWS)
                    b_out = _pad_rows(b_o.reshape(dec_batch, b_w), SAMPLE_ROWS)
                    for key, val in (("aks", ka_h), ("avs", va_h), ("bks", kb_h), ("bvs", vb_h)):
                        outs[key].append(val[:, None])
                x = out_proj([a_out, b_out], "ab_w_out", ab_w_out, e, x)
            else:
                o = i // 2
                (q,) = proj(h, "c_w_in", c_w_in, o, 0, c_w, [BF16])
                (k,) = proj(h, "c_w_in", c_w_in, o, c_w, c_w, [F32])
                (v,) = proj(h, "c_w_in", c_w_in, o, 2 * c_w, c_w, [F32])
                w_f = jnp.pad(c_w_in[o][:, 3 * c_w:], ((0, 0), (0, LANES - c_heads)))
                b_f = jnp.pad(c_b_f[o], (0, LANES - c_heads)).reshape(1, LANES)
                logf = _logf_proj(h, w_f, b_f)[:, :c_heads]
                if is_prompt:
                    c = _cumsum_time(logf.reshape(batch, seq, c_heads)).transpose(0, 2, 1)
                    hs = _head_spec(seq)
                    c_out = _prompt_attn(
                        _fox_prompt_kernel, "fox_prompt", batch, seq, c_heads,
                        [q, k, v, c.reshape(batch, c_heads, seq, 1)],
                        [hs] * 3 + [pl.BlockSpec((1, 1, seq, 1), lambda b, hh: (b, hh, 0, 0))])
                    outs["ckp"].append(k.reshape(batch, seq, c_heads, HEAD_DIM))
                    outs["cvp"].append(v.reshape(batch, seq, c_heads, HEAD_DIM))
                    outs["cfp"].append(logf.reshape(batch, seq, c_heads))
                else:
                    k_h, v_h = sample_heads(k, c_heads), sample_heads(v, c_heads)
                    lf_new = logf[:dec_batch]
                    bias = _suffix_bias(cache_c_logf, o, page_table, lf_new)
                    bias = bias.reshape(dec_batch, page_table.shape[1], 1, page * c_heads)
                    c_o = _decode_attn("fox", sample_heads(q, c_heads), cache_c_k, cache_c_v, o, page_table,
                                       k_h, v_h, n_pg=4, scale=HEAD_DIM ** -0.5, bias=bias)
                    c_out = _pad_rows(c_o.reshape(dec_batch, c_w), SAMPLE_ROWS)
                    outs["cks"].append(k_h[:, None])
                    outs["cvs"].append(v_h[:, None])
                    outs["cfs"].append(lf_new[:, None])
                x = out_proj([c_out], "c_w_out", c_w_out, o, x)
            x = ffn(x, ffn2_norm[i], [("ffn2_w_gate", ffn2_w_gate), ("ffn2_w_up", ffn2_w_up),
                                      ("ffn2_w_down", ffn2_w_down)], i)
            hp = _rmsnorm(x, ple_norm[i], BF16)
            p2d = p_in[i].reshape(-1, ple_dim)
            if not is_prompt:
                p2d = _pad_rows(p2d, SAMPLE_ROWS)
            x, made = _ple(hp, weight("ple_w_gate", ple_w_gate, i), p2d, ple_w_proj, i, x)
            if made is not None:
                copies[("ple_w_gate", i, 0)] = _W(made)
        final[is_prompt] = x

    y_prompt = _rmsnorm(final[True], final_norm, F32).reshape(batch, seq, d_model)
    y_sample = _rmsnorm(final[False], final_norm, F32)[:dec_batch].reshape(dec_batch, dec_seq, d_model)
    st = {k: jnp.stack(v) for k, v in outs.items()}
    return (y_prompt, y_sample,
            st["akp"], st["avp"], st["aks"], st["avs"],
            st["bkp"], st["bvp"], st["bks"], st["bvs"],
            st["ckp"], st["cvp"], st["cfp"], st["cks"], st["cvs"], st["cfs"])
```

```python
import functools
import math
from typing import NamedTuple, Optional

import jax
import jax.numpy as jnp
from jax import lax
from jax.experimental import pallas as pl
from jax.experimental.pallas import tpu as pltpu

F32 = jnp.float32
BF16 = jnp.bfloat16

HEAD_DIM = 128
MOBA_BLOCK = 256
MOBA_TOPK = 3
ROPE_THETA = 10000.0
NORM_EPS = 1e-6
SUBLN_EPS = 1e-5
NEG = -1e30

V7X_VMEM_BYTES = 64 * 1024 * 1024
VMEM_LIMIT = V7X_VMEM_BYTES - 8 * 1024 * 1024
LANES = 128
SAMPLE_ROWS = 16


def _params(*sem):
    return pltpu.CompilerParams(dimension_semantics=sem, vmem_limit_bytes=VMEM_LIMIT)


def _split3(x):
    a = x.astype(BF16)
    r = x - a.astype(F32)
    b = r.astype(BF16)
    c = (r - b.astype(F32)).astype(BF16)
    return a, b, c


def _dot(a, b):
    return jnp.dot(a, b, preferred_element_type=F32)


def _dot_nt(a, b):
    return lax.dot_general(a, b, (((1,), (1,)), ((), ())), preferred_element_type=F32)


class _W(NamedTuple):
    arr: jax.Array
    layer: Optional[int] = None


def _w_spec(w, rows, cols, index):
    if w.layer is None:
        return pl.BlockSpec((rows, cols), index)
    return pl.BlockSpec((1, rows, cols), lambda *g: (w.layer,) + tuple(index(*g)))


def _w_tile(ref):
    tile = ref[...] if len(ref.shape) == 2 else ref[0]
    return tile.astype(BF16)


def _rms(x, g, eps):
    return x * lax.rsqrt(jnp.mean(x * x, axis=-1, keepdims=True) + eps) * g


def _rmsnorm_kernel(x_ref, g_ref, o_ref):
    o_ref[...] = _rms(x_ref[...], g_ref[...], NORM_EPS).astype(o_ref.dtype)


def _rmsnorm(x, g, out_dtype):
    m, d = x.shape
    bm = min(512, m)
    return pl.pallas_call(
        _rmsnorm_kernel,
        grid=(m // bm,),
        in_specs=[pl.BlockSpec((bm, d), lambda i: (i, 0)),
                  pl.BlockSpec((1, d), lambda i: (0, 0))],
        out_specs=pl.BlockSpec((bm, d), lambda i: (i, 0)),
        out_shape=jax.ShapeDtypeStruct((m, d), out_dtype),
        compiler_params=_params("parallel"),
        name="rmsnorm",
    )(x, g.reshape(1, d))


FFN_SLAB = 256


def _ffn_kernel(*refs, emit):
    x_ref, g_ref, wg_ref, wu_ref, wd_ref, o_ref = refs[:6]
    xn_ref = refs[-1]

    @pl.when(pl.program_id(1) == 0)
    def _():
        x = x_ref[...]
        xn_ref[...] = _rms(x, g_ref[...], NORM_EPS).astype(BF16)
        o_ref[...] = x

    tiles = [_w_tile(r) for r in (wg_ref, wu_ref, wd_ref)]
    if emit:
        for copy_ref, tile in zip(refs[6:9], tiles):
            copy_ref[...] = tile
    xn = xn_ref[...]
    gate = _dot(xn, tiles[0])
    up = _dot(xn, tiles[1])
    act = (0.5 * (gate * jax.nn.sigmoid(gate)) * up).astype(BF16)
    o_ref[...] += _dot(act, tiles[2])


def _ffn(x, g, wg, wu, wd):
    m, d = x.shape
    f = wg.arr.shape[-1]
    bm = min(512, m)
    emit = wg.layer is not None
    assert not emit or m == bm
    slab_cols = lambda i, j: (0, j)
    slab_rows = lambda i, j: (j, 0)
    out_specs = [pl.BlockSpec((bm, d), lambda i, j: (i, 0))]
    out_shape = [jax.ShapeDtypeStruct((m, d), F32)]
    if emit:
        out_specs += [pl.BlockSpec((d, FFN_SLAB), slab_cols)] * 2 + [pl.BlockSpec((FFN_SLAB, d), slab_rows)]
        out_shape += [jax.ShapeDtypeStruct((d, f), BF16)] * 2 + [jax.ShapeDtypeStruct((f, d), BF16)]
    res = pl.pallas_call(
        functools.partial(_ffn_kernel, emit=emit),
        grid=(m // bm, f // FFN_SLAB),
        in_specs=[pl.BlockSpec((bm, d), lambda i, j: (i, 0), pipeline_mode=pl.Buffered(1)),
                  pl.BlockSpec((1, d), lambda i, j: (0, 0)),
                  _w_spec(wg, d, FFN_SLAB, slab_cols),
                  _w_spec(wu, d, FFN_SLAB, slab_cols),
                  _w_spec(wd, FFN_SLAB, d, slab_rows)],
        out_specs=out_specs,
        out_shape=out_shape,
        scratch_shapes=[pltpu.VMEM((bm, d), BF16)],
        compiler_params=_params("parallel", "arbitrary"),
        name="ffn",
    )(x, g.reshape(1, d), wg.arr, wu.arr, wd.arr)
    return res[0], (tuple(res[1:]) if emit else None)


def _rope_tables(pos, kind):
    width = HEAD_DIM if kind == "a" else HEAD_DIM // 2
    half = width // 2
    inv_freq = jnp.power(ROPE_THETA, -jnp.arange(half, dtype=F32) / half)
    ang = pos.astype(F32)[:, None] * inv_freq[None, :]
    cos, sin = jnp.cos(ang), jnp.sin(ang)
    reps = HEAD_DIM // width
    cos_t = jnp.tile(jnp.concatenate([cos, cos], axis=-1), (1, reps))
    sin_t = jnp.tile(jnp.concatenate([-sin, sin], axis=-1), (1, reps))
    return cos_t, sin_t


def _rope_head(x, cos, sin, kind):
    if kind == "a":
        partner = pltpu.roll(x, HEAD_DIM // 2, 1)
    else:
        lane = lax.broadcasted_iota(jnp.int32, x.shape, 1)
        first = (lane & (HEAD_DIM // 2 - 1)) < (HEAD_DIM // 4)
        partner = jnp.where(first, pltpu.roll(x, HEAD_DIM - HEAD_DIM // 4, 1), pltpu.roll(x, HEAD_DIM // 4, 1))
    return x * cos + partner * sin


def _proj_kernel(*refs, rope, n_out, emit):
    if rope is None:
        a_ref, w_ref = refs[:2]
        outs = refs[2:]
    else:
        a_ref, w_ref, cos_ref, sin_ref = refs[:4]
        outs = refs[4:]
    tile = _w_tile(w_ref)
    if emit:
        outs[n_out][...] = tile
    acc = _dot(a_ref[...], tile)
    if rope is None:
        for o in outs[:n_out]:
            o[...] = acc.astype(o.dtype)
        return
    cos, sin = cos_ref[...], sin_ref[...]
    for h in range(acc.shape[1] // HEAD_DIM):
        cols = slice(h * HEAD_DIM, (h + 1) * HEAD_DIM)
        head = _rope_head(acc[:, cols], cos, sin, rope)
        for o in outs[:n_out]:
            o[:, cols] = head.astype(o.dtype)


def _proj(a, w, col0, width, out_dtypes, rope=None, tables=None):
    m, k = a.shape
    bm = min(1024, m) if rope is None else min(1024, m, tables[0].shape[0])
    bn = min(512, width)
    emit = w.layer is not None
    assert col0 % bn == 0 and width % bn == 0 and (not emit or m == bm)
    first = col0 // bn if emit else 0
    in_specs = [pl.BlockSpec((bm, k), lambda i, j: (i, 0)),
                _w_spec(w, k, bn, lambda i, j: (0, first + j))]
    args = [a, w.arr]
    if rope is not None:
        nt = tables[0].shape[0] // bm
        in_specs += [pl.BlockSpec((bm, HEAD_DIM), lambda i, j: (i % nt, 0))] * 2
        args += list(tables)
    out_specs = [pl.BlockSpec((bm, bn), lambda i, j: (i, j))] * len(out_dtypes)
    out_shape = [jax.ShapeDtypeStruct((m, width), dt) for dt in out_dtypes]
    if emit:
        out_specs.append(pl.BlockSpec((k, bn), lambda i, j: (0, j)))
        out_shape.append(jax.ShapeDtypeStruct((k, width), BF16))
    outs = pl.pallas_call(
        functools.partial(_proj_kernel, rope=rope, n_out=len(out_dtypes), emit=emit),
        grid=(m // bm, width // bn),
        in_specs=in_specs,
        out_specs=out_specs,
        out_shape=out_shape,
        compiler_params=_params("parallel", "parallel"),
        name="proj_" + (rope or "plain"),
    )(*args)
    return (outs[:-1], outs[-1]) if emit else (outs, None)


def _logf_kernel(a_ref, w_ref, b_ref, o_ref):
    z = _dot(a_ref[...], w_ref[...].astype(BF16)) + b_ref[...]
    o_ref[...] = jnp.minimum(z, 0.0) - jnp.log1p(jnp.exp(-jnp.abs(z)))


def _logf_proj(a, w_f, b_f):
    m, k = a.shape
    bm = min(1024, m)
    return pl.pallas_call(
        _logf_kernel,
        grid=(m // bm,),
        in_specs=[pl.BlockSpec((bm, k), lambda i: (i, 0)),
                  pl.BlockSpec((k, LANES), lambda i: (0, 0)),
                  pl.BlockSpec((1, LANES), lambda i: (0, 0))],
        out_specs=pl.BlockSpec((bm, LANES), lambda i: (i, 0)),
        out_shape=jax.ShapeDtypeStruct((m, LANES), F32),
        compiler_params=_params("parallel"),
        name="logf_proj",
    )(a, w_f, b_f)


def _out_proj_kernel(*refs, n_lhs, emit):
    a_refs = refs[:n_lhs]
    w_ref, x_ref, o_ref = refs[n_lhs:n_lhs + 3]
    tile = _w_tile(w_ref)
    if emit:
        refs[n_lhs + 3][...] = tile
    acc = x_ref[...]
    k0 = 0
    for a_ref in a_refs:
        kk = a_ref.shape[1]
        acc = acc + _dot(a_ref[...], tile[k0:k0 + kk, :])
        k0 += kk
    o_ref[...] = acc


def _tiled_call(kernel, name, m, n, in_specs, args, emit, k):
    bm, bn = min(1024, m), min(512, n)
    assert not emit or m == bm
    out_specs = [pl.BlockSpec((bm, bn), lambda i, j: (i, j))]
    out_shape = [jax.ShapeDtypeStruct((m, n), F32)]
    if emit:
        out_specs.append(pl.BlockSpec((k, bn), lambda i, j: (0, j)))
        out_shape.append(jax.ShapeDtypeStruct((k, n), BF16))
    res = pl.pallas_call(
        kernel,
        grid=(m // bm, n // bn),
        in_specs=in_specs(bm, bn),
        out_specs=out_specs,
        out_shape=out_shape,
        compiler_params=_params("parallel", "parallel"),
        name=name,
    )(*args)
    return res[0], (res[1] if emit else None)


def _out_proj(lhs, w, x):
    m, n = x.shape
    k = w.arr.shape[-2]
    emit = w.layer is not None

    def in_specs(bm, bn):
        return ([pl.BlockSpec((bm, a.shape[1]), lambda i, j: (i, 0)) for a in lhs]
                + [_w_spec(w, k, bn, lambda i, j: (0, j)), pl.BlockSpec((bm, bn), lambda i, j: (i, j))])

    return _tiled_call(functools.partial(_out_proj_kernel, n_lhs=len(lhs), emit=emit), "out_proj", m, n,
                       in_specs, [*lhs, w.arr, x], emit, k)


def _ple_kernel(*refs, emit):
    a_ref, wg_ref, p_ref, wp_ref, x_ref, o_ref = refs[:6]
    tile = _w_tile(wg_ref)
    if emit:
        refs[6][...] = tile
    gate = jax.nn.sigmoid(_dot(a_ref[...], tile))
    emb = _dot(p_ref[...].astype(BF16), wp_ref[0].astype(BF16))
    o_ref[...] = x_ref[...] + gate * emb


def _ple(a, wg, p, wp, layer, x):
    m, n = x.shape
    k = a.shape[1]
    kp = p.shape[1]
    emit = wg.layer is not None

    def in_specs(bm, bn):
        return [pl.BlockSpec((bm, k), lambda i, j: (i, 0)),
                _w_spec(wg, k, bn, lambda i, j: (0, j)),
                pl.BlockSpec((bm, kp), lambda i, j: (i, 0)),
                pl.BlockSpec((1, kp, bn), lambda i, j: (layer, 0, j)),
                pl.BlockSpec((bm, bn), lambda i, j: (i, j))]

    return _tiled_call(functools.partial(_ple_kernel, emit=emit), "ple", m, n, in_specs, [a, wg.arr, p, wp, x],
                       emit, k)


ATTN_TQ = 256


def _diag_mask(n_rows):
    r = lax.broadcasted_iota(jnp.int32, (n_rows, ATTN_TQ), 0) & (ATTN_TQ - 1)
    c = lax.broadcasted_iota(jnp.int32, (n_rows, ATTN_TQ), 1)
    return c <= r


def _causal_exp(t_raw, i, scale):
    kprev = i * ATTN_TQ
    c = scale * math.log2(math.e)
    t_d = jnp.where(_diag_mask(t_raw.shape[0]), t_raw[:, kprev:], NEG)
    m = jnp.max(t_d, axis=-1, keepdims=True)
    if i:
        t_p = t_raw[:, :kprev]
        m = jnp.maximum(m, jnp.max(t_p, axis=-1, keepdims=True))
    e_d = jnp.exp2((t_d - m) * c).astype(BF16)
    return (jnp.exp2((t_p - m) * c).astype(BF16) if i else None), e_d


def _values_and_ones(v_ref):
    v = v_ref[...].astype(BF16)
    return jnp.concatenate([v, jnp.ones_like(v)], axis=1)


def _pv(e_prev, e_diag, v_ones, i):
    kprev = i * ATTN_TQ
    o = _dot(e_diag, v_ones[kprev:kprev + ATTN_TQ, :])
    if i:
        o = o + _dot(e_prev, v_ones[0:kprev, :])
    return o[:, 0:HEAD_DIM] / o[:, HEAD_DIM:HEAD_DIM + 1]


def _moba_prompt_kernel(q_ref, k_ref, v_ref, o_ref):
    t = q_ref.shape[0]
    nb = t // MOBA_BLOCK
    assert ATTN_TQ == MOBA_BLOCK and nb <= LANES
    scale = HEAD_DIM ** -0.5
    km = jnp.sum(k_ref[...].reshape(nb, MOBA_BLOCK, HEAD_DIM), axis=1) * (1.0 / MOBA_BLOCK)
    km = jnp.concatenate([km, jnp.zeros((LANES - nb, HEAD_DIM), F32)], axis=0)
    km_hi = km.astype(BF16)
    km_lo = (km - km_hi.astype(F32)).astype(BF16)
    key_blk = lax.broadcasted_iota(jnp.int32, (t, LANES), 0) >> (MOBA_BLOCK.bit_length() - 1)
    onehot = (key_blk == lax.broadcasted_iota(jnp.int32, (t, LANES), 1)).astype(F32).astype(BF16)
    k_aug = jnp.concatenate([k_ref[...].astype(BF16), onehot], axis=1)
    v_ones = _values_and_ones(v_ref)
    for i in range(nb):
        rows = slice(i * ATTN_TQ, (i + 1) * ATTN_TQ)
        kend = (i + 1) * MOBA_BLOCK
        q = q_ref[rows, :]
        if i > MOBA_TOPK:
            gate = _dot_nt(q, km_hi) + _dot_nt(q, km_lo)
            lane = lax.broadcasted_iota(jnp.int32, gate.shape, 1)
            rank = jnp.zeros(gate.shape, F32)
            for m in range(i):
                g_m = gate[:, m:m + 1]
                ahead = (g_m > gate) | ((g_m == gate) & (lane > m))
                rank = rank + ahead.astype(F32)
            bias = jnp.where((rank < MOBA_TOPK) | (lane >= i), 0.0, NEG)
            q_aug = jnp.concatenate([q, bias.astype(BF16)], axis=1)
            t_raw = _dot_nt(q_aug, k_aug[0:kend, :])
        else:
            t_raw = _dot_nt(q, k_aug[0:kend, 0:HEAD_DIM])
        o_ref[rows, :] = _pv(*_causal_exp(t_raw, i, scale), v_ones, i).astype(o_ref.dtype)


def _diff_lambda(lq1, lk1, lq2, lk2, lam_init):
    return (jnp.exp(jnp.sum(lq1 * lk1, axis=-1, keepdims=True))
            - jnp.exp(jnp.sum(lq2 * lk2, axis=-1, keepdims=True)) + lam_init)


def _diff_finish(o, subln, lam_init):
    return _rms(o, subln, SUBLN_EPS) * (1.0 - lam_init)


def _diff_prompt_kernel(q_ref, k_ref, v_ref, lq1_ref, lk1_ref, lq2_ref, lk2_ref, g_ref, o_ref, *, lam_init):
    t = q_ref.shape[0]
    half = HEAD_DIM // 2
    scale = half ** -0.5
    lam = _diff_lambda(lq1_ref[...], lk1_ref[...], lq2_ref[...], lk2_ref[...], lam_init)
    v_ones = _values_and_ones(v_ref)
    lane = lax.broadcasted_iota(jnp.int32, (ATTN_TQ, HEAD_DIM), 1)
    for i in range(t // ATTN_TQ):
        rows = slice(i * ATTN_TQ, (i + 1) * ATTN_TQ)
        kend = (i + 1) * ATTN_TQ
        q = q_ref[rows, :]
        k = k_ref[0:kend, :].astype(BF16)
        zero = jnp.zeros_like(q)
        q12 = jnp.concatenate([jnp.where(lane < half, q, zero), jnp.where(lane >= half, q, zero)], axis=0)
        o12 = _pv(*_causal_exp(_dot_nt(q12, k), i, scale), v_ones, i)
        o = o12[0:ATTN_TQ] - lam * o12[ATTN_TQ:2 * ATTN_TQ]
        o_ref[rows, :] = _diff_finish(o, g_ref[...], lam_init).astype(o_ref.dtype)


def _fox_prompt_kernel(q_ref, k_ref, v_ref, ccol_ref, o_ref):
    t = q_ref.shape[0]
    scale = HEAD_DIM ** -0.5
    hi, mid, lo = (term.astype(F32) for term in _split3(ccol_ref[0, 0] * (1.0 / scale)))
    lane = lax.broadcasted_iota(jnp.int32, (t, LANES), 1)
    one = jnp.ones((t, LANES), F32)

    def features(vals):
        out = jnp.zeros((t, LANES), F32)
        for n, val in enumerate(vals):
            out = jnp.where(lane == n, val, out)
        return out.astype(BF16)

    q_feat = features([hi, mid, lo, one, one, one])
    k_aug = jnp.concatenate([k_ref[...].astype(BF16), features([one, one, one, -hi, -mid, -lo])], axis=1)
    v_ones = _values_and_ones(v_ref)
    for i in range(t // ATTN_TQ):
        rows = slice(i * ATTN_TQ, (i + 1) * ATTN_TQ)
        kend = (i + 1) * ATTN_TQ
        q_aug = jnp.concatenate([q_ref[rows, :], q_feat[rows, :]], axis=1)
        o_ref[rows, :] = _pv(*_causal_exp(_dot_nt(q_aug, k_aug[0:kend, :]), i, scale), v_ones, i).astype(o_ref.dtype)


def _head_spec(t):
    return pl.BlockSpec((t, HEAD_DIM), lambda b, h: (b, h))


def _vec_spec(n):
    return pl.BlockSpec((1, n), lambda b, h: (0, 0))


def _prompt_attn(kernel, name, batch, t, heads, args, in_specs):
    return pl.pallas_call(
        kernel,
        grid=(batch, heads),
        in_specs=in_specs,
        out_specs=_head_spec(t),
        out_shape=jax.ShapeDtypeStruct((batch * t, heads * HEAD_DIM), BF16),
        compiler_params=_params("parallel", "parallel"),
        name=name,
    )(*args)


def _tri_dot(tri, x):
    a, b, c = _split3(x)
    return _dot(tri, a) + _dot(tri, b) + _dot(tri, c)


CUMSUM_CHUNK = 256


def _cumsum_kernel(lf_ref, o_ref):
    t = lf_ref.shape[1]
    r = lax.broadcasted_iota(jnp.int32, (CUMSUM_CHUNK, CUMSUM_CHUNK), 0)
    c = lax.broadcasted_iota(jnp.int32, (CUMSUM_CHUNK, CUMSUM_CHUNK), 1)
    tri = (c <= r).astype(BF16)
    carry = jnp.zeros((1, lf_ref.shape[2]), F32)
    for j in range(t // CUMSUM_CHUNK):
        rows = slice(j * CUMSUM_CHUNK, (j + 1) * CUMSUM_CHUNK)
        cs = _tri_dot(tri, lf_ref[0, rows, :]) + carry
        o_ref[0, rows, :] = cs
        carry = cs[CUMSUM_CHUNK - 1:CUMSUM_CHUNK, :]


def _cumsum_time(lf):
    b, t, h = lf.shape
    return pl.pallas_call(
        _cumsum_kernel,
        grid=(b,),
        in_specs=[pl.BlockSpec((1, t, h), lambda i: (i, 0, 0))],
        out_specs=pl.BlockSpec((1, t, h), lambda i: (i, 0, 0)),
        out_shape=jax.ShapeDtypeStruct(lf.shape, F32),
        compiler_params=_params("parallel"),
        name="cumsum_time",
    )(lf)


SUFFIX_PAGES = 8


def _suffix_kernel(*refs, n_pg):
    lf_refs = refs[1:1 + n_pg]
    new_ref, o_ref, carry_ref = refs[1 + n_pg:]

    @pl.when(pl.program_id(1) == 0)
    def _():
        carry_ref[...] = new_ref[0]

    page = lf_refs[0].shape[2]
    r = lax.broadcasted_iota(jnp.int32, (page, page), 0)
    c = lax.broadcasted_iota(jnp.int32, (page, page), 1)
    tri = (c > r).astype(BF16)
    carry = carry_ref[...]
    for g in reversed(range(n_pg)):
        lf = lf_refs[g][0, 0]
        o_ref[0, g] = _tri_dot(tri, lf) + carry
        carry = carry + jnp.sum(lf, axis=0, keepdims=True)
    carry_ref[...] = carry


def _suffix_bias(cache_lf, layer, page_table, lf_new):
    bsz, n_pages = page_table.shape
    page, h = cache_lf.shape[2], cache_lf.shape[3]
    n_pg = math.gcd(SUFFIX_PAGES, n_pages)
    steps = n_pages // n_pg

    def spec(g):
        return pl.BlockSpec((1, 1, page, h), lambda b, j, pt: (layer, pt[b, (steps - 1 - j) * n_pg + g], 0, 0))

    grid_spec = pltpu.PrefetchScalarGridSpec(
        num_scalar_prefetch=1,
        grid=(bsz, steps),
        in_specs=[spec(g) for g in range(n_pg)] + [pl.BlockSpec((1, 1, h), lambda b, j, pt: (b, 0, 0))],
        out_specs=pl.BlockSpec((1, n_pg, page, h), lambda b, j, pt: (b, steps - 1 - j, 0, 0)),
        scratch_shapes=[pltpu.VMEM((1, h), F32)],
    )
    return pl.pallas_call(
        functools.partial(_suffix_kernel, n_pg=n_pg),
        grid_spec=grid_spec,
        out_shape=jax.ShapeDtypeStruct((bsz, n_pages, page, h), F32),
        compiler_params=_params("parallel", "arbitrary"),
        name="suffix_bias",
    )(page_table, *([cache_lf] * n_pg), lf_new.reshape(bsz, 1, h))


KMEAN_BLOCKS = 4


def _kmean_kernel(*refs, per, blocks):
    k_refs, o_ref = refs[1:-1], refs[-1]
    for n in range(blocks):
        tot = jnp.sum(k_refs[per * n][0, 0], axis=0)
        for g in range(1, per):
            tot = tot + jnp.sum(k_refs[per * n + g][0, 0], axis=0)
        o_ref[0, n] = tot * (1.0 / MOBA_BLOCK)


def _kmean_pages(cache_k, layer, page_table):
    bsz, n_pages = page_table.shape
    page, h, d = cache_k.shape[2:]
    per = MOBA_BLOCK // page
    nblk = n_pages // per
    blocks = math.gcd(KMEAN_BLOCKS, nblk)
    n_pg = per * blocks

    def spec(g):
        return pl.BlockSpec((1, 1, page, h, d), lambda b, n, pt: (layer, pt[b, n_pg * n + g], 0, 0, 0))

    grid_spec = pltpu.PrefetchScalarGridSpec(
        num_scalar_prefetch=1,
        grid=(bsz, nblk // blocks),
        in_specs=[spec(g) for g in range(n_pg)],
        out_specs=pl.BlockSpec((1, blocks, h, d), lambda b, n, pt: (b, n, 0, 0)),
    )
    return pl.pallas_call(
        functools.partial(_kmean_kernel, per=per, blocks=blocks),
        grid_spec=grid_spec,
        out_shape=jax.ShapeDtypeStruct((bsz, nblk, h, d), F32),
        compiler_params=_params("parallel", "parallel"),
        name="kmean_pages",
    )(page_table, *([cache_k] * n_pg))


def _moba_select_kernel(km_ref, q_ref, o_ref):
    km = km_ref[0]
    gate = jnp.sum(km * q_ref[...], axis=-1, keepdims=True)
    nblk = gate.shape[0]
    idx = lax.broadcasted_iota(jnp.int32, gate.shape, 0).astype(F32)
    for s in range(MOBA_TOPK):
        best = jnp.max(gate, axis=0, keepdims=True)
        first = jnp.min(jnp.where(gate == best, idx, float(nblk)), axis=0, keepdims=True)
        o_ref[0, s] = first[0]
        gate = jnp.where(idx == first, -jnp.inf, gate)


def _moba_select(kmean, q):
    bsz, nblk, h, d = kmean.shape
    assert nblk >= MOBA_TOPK
    picks = pl.pallas_call(
        _moba_select_kernel,
        grid=(bsz,),
        in_specs=[pl.BlockSpec((1, nblk, h, d), lambda b: (b, 0, 0, 0)),
                  pl.BlockSpec((1, h, d), lambda b: (b, 0, 0))],
        out_specs=pl.BlockSpec((1, MOBA_TOPK, h, 1), lambda b: (b, 0, 0, 0)),
        out_shape=jax.ShapeDtypeStruct((bsz, MOBA_TOPK, h, 1), F32),
        compiler_params=_params("parallel"),
        name="moba_select",
    )(kmean, q)
    return picks[..., 0].astype(jnp.int32)


def _moba_decode_kernel(pt_ref, sel_ref, q_ref, k0_ref, k1_ref, v0_ref, v1_ref, knew_ref, vnew_ref, o_ref,
                        m_ref, l_ref, acc_ref, *, scale):
    s_id = pl.program_id(2)

    @pl.when(s_id == 0)
    def _():
        m_ref[...] = jnp.full(m_ref.shape, NEG, F32)
        l_ref[...] = jnp.zeros(l_ref.shape, F32)
        acc_ref[...] = jnp.zeros(acc_ref.shape, F32)

    q = q_ref[0, 0]
    page, grp = k0_ref.shape[2], k0_ref.shape[3]
    width = page * grp
    lane = lax.broadcasted_iota(jnp.int32, (1, width), 1)
    mine = (lane & (grp - 1)) == (pl.program_id(1) & (grp - 1))

    def update(scores, oks, pvs):
        m_old = m_ref[...]
        m_new = m_old
        for s in scores:
            m_new = jnp.maximum(m_new, jnp.max(s, axis=-1, keepdims=True))
        alpha = jnp.exp(m_old - m_new)
        l = alpha * l_ref[...]
        acc = alpha * acc_ref[...]
        for s, ok, pv in zip(scores, oks, pvs):
            p = jnp.exp(s - m_new)
            if ok is not None:
                p = jnp.where(ok, p, 0.0)
            l = l + jnp.sum(p, axis=-1, keepdims=True)
            acc = acc + pv(p.astype(BF16))
        l_ref[...] = l
        acc_ref[...] = acc
        m_ref[...] = m_new

    pages = ((k0_ref, v0_ref), (k1_ref, v1_ref))
    update([jnp.where(mine, _dot_nt(q, k_ref[0, 0].reshape(width, HEAD_DIM).astype(BF16)) * scale, NEG)
            for k_ref, _ in pages],
           [mine] * len(pages),
           [lambda p, v_ref=v_ref: _dot(p, v_ref[0, 0].reshape(width, HEAD_DIM).astype(BF16)) for _, v_ref in pages])

    @pl.when(s_id == pl.num_programs(2) - 1)
    def _():
        knew = knew_ref[0, 0].astype(BF16).astype(F32)
        vnew = vnew_ref[0, 0].astype(BF16).astype(F32)
        update([jnp.sum(q.astype(F32) * knew, axis=-1, keepdims=True) * scale], [None],
               [lambda p: p.astype(F32) * vnew])
        o_ref[0, 0] = (acc_ref[...] / l_ref[...]).astype(o_ref.dtype)


MOBA_HEAD_GROUP = 8


def _moba_decode(q, cache_k, cache_v, layer, page_table, picks, k_new, v_new):
    bsz, heads, d = q.shape
    page = cache_k.shape[2]
    per = MOBA_BLOCK // page
    grp = min(MOBA_HEAD_GROUP, heads)
    assert per == 2 and heads % grp == 0

    def page_spec(g):
        return pl.BlockSpec((1, 1, page, grp, d),
                            lambda b, h, s, pt, sel: (layer, pt[b, per * sel[b, s, h] + g], 0, h // grp, 0))

    row = pl.BlockSpec((1, 1, 1, d), lambda b, h, s, pt, sel: (b, h, 0, 0))
    grid_spec = pltpu.PrefetchScalarGridSpec(
        num_scalar_prefetch=2,
        grid=(bsz, heads, picks.shape[1]),
        in_specs=[row, page_spec(0), page_spec(1), page_spec(0), page_spec(1), row, row],
        out_specs=row,
        scratch_shapes=[pltpu.VMEM((1, 1), F32), pltpu.VMEM((1, 1), F32), pltpu.VMEM((1, d), F32)],
    )
    out = pl.pallas_call(
        functools.partial(_moba_decode_kernel, scale=HEAD_DIM ** -0.5),
        grid_spec=grid_spec,
        out_shape=jax.ShapeDtypeStruct((bsz, heads, 1, d), BF16),
        compiler_params=_params("parallel", "parallel", "arbitrary"),
        name="decode_moba",
    )(page_table, picks, q.reshape(bsz, heads, 1, d), cache_k, cache_k, cache_v, cache_v,
      k_new.reshape(bsz, heads, 1, d), v_new.reshape(bsz, heads, 1, d))
    return out.reshape(bsz, heads, d)


def _decode_kernel(*refs, mode, n_pg, heads, scale, lam_init):
    pt_ref, q_ref = refs[0], refs[1]
    k_refs = refs[2:2 + n_pg]
    v_refs = refs[2 + n_pg:2 + 2 * n_pg]
    pos = 2 + 2 * n_pg
    if mode == "fox":
        bias_refs = refs[pos:pos + n_pg]
        pos += n_pg
    knew_ref, vnew_ref = refs[pos], refs[pos + 1]
    pos += 2
    if mode == "diff":
        lq1_ref, lk1_ref, lq2_ref, lk2_ref, g_ref = refs[pos:pos + 5]
        pos += 5
    o_ref, m_ref, l_ref, acc_ref = refs[pos:pos + 4]

    j = pl.program_id(1)

    @pl.when(j == 0)
    def _():
        m_ref[...] = jnp.full(m_ref.shape, NEG, F32)
        l_ref[...] = jnp.zeros(l_ref.shape, F32)
        acc_ref[...] = jnp.zeros(acc_ref.shape, F32)

    q = q_ref[0]
    nrow = q.shape[0]
    page = k_refs[0].shape[2]
    width = page * heads
    lane = lax.broadcasted_iota(jnp.int32, (nrow, width), 1)
    row = lax.broadcasted_iota(jnp.int32, (nrow, width), 0)
    assert heads & (heads - 1) == 0
    ok = (lane & (heads - 1)) == (row & (heads - 1))

    scores = []
    for g in range(n_pg):
        kf = k_refs[g][0, 0].reshape(width, HEAD_DIM).astype(BF16)
        s = _dot_nt(q, kf) * scale
        if mode == "fox":
            s = s + bias_refs[g][0, 0]
        scores.append(jnp.where(ok, s, NEG))
    m_old = m_ref[...]
    m_new = m_old
    for s in scores:
        m_new = jnp.maximum(m_new, jnp.max(s, axis=-1, keepdims=True))
    alpha = jnp.exp(m_old - m_new)
    l = alpha * l_ref[...]
    acc = alpha * acc_ref[...]
    for g, s in enumerate(scores):
        p = jnp.where(ok, jnp.exp(s - m_new), 0.0)
        l = l + jnp.sum(p, axis=-1, keepdims=True)
        acc = acc + _dot(p.astype(BF16), v_refs[g][0, 0].reshape(width, HEAD_DIM).astype(BF16))
    m_ref[...] = m_new
    l_ref[...] = l
    acc_ref[...] = acc

    @pl.when(j == pl.num_programs(1) - 1)
    def _():
        reps = nrow // heads
        knew = knew_ref[0].astype(BF16).astype(F32)
        vnew = vnew_ref[0].astype(BF16).astype(F32)
        if reps > 1:
            knew = jnp.concatenate([knew] * reps, axis=0)
            vnew = jnp.concatenate([vnew] * reps, axis=0)
        s_new = jnp.sum(q.astype(F32) * knew, axis=-1, keepdims=True) * scale
        m_old = m_ref[...]
        m_new = jnp.maximum(m_old, s_new)
        alpha = jnp.exp(m_old - m_new)
        p_new = jnp.exp(s_new - m_new)
        l = alpha * l_ref[...] + p_new
        acc = alpha * acc_ref[...] + p_new.astype(BF16).astype(F32) * vnew
        o = acc / l
        if mode == "diff":
            lam = _diff_lambda(lq1_ref[...], lk1_ref[...], lq2_ref[...], lk2_ref[...], lam_init)
            o = _diff_finish(o[0:heads] - lam * o[heads:2 * heads], g_ref[...], lam_init)
        o_ref[0] = o.astype(o_ref.dtype)


def _decode_attn(mode, q, cache_k, cache_v, layer, page_table, k_new, v_new, *, n_pg, scale,
                 bias=None, diff_params=None, lam_init=0.0):
    bsz, n_pages = page_table.shape
    page, heads, d = cache_k.shape[2:]
    nrow = q.shape[1]
    n_pg = math.gcd(n_pg, n_pages)
    steps = n_pages // n_pg

    def page_spec(g):
        return pl.BlockSpec((1, 1, page, heads, d), lambda b, j, pt: (layer, pt[b, n_pg * j + g], 0, 0, 0))

    in_specs = [pl.BlockSpec((1, nrow, d), lambda b, j, pt: (b, 0, 0))]
    in_specs += [page_spec(g) for g in range(n_pg)] * 2
    args = [q] + [cache_k] * n_pg + [cache_v] * n_pg
    if mode == "fox":
        for g in range(n_pg):
            in_specs.append(pl.BlockSpec((1, 1, 1, page * heads),
                                         functools.partial(lambda b, j, pt, g: (b, n_pg * j + g, 0, 0), g=g)))
        args += [bias] * n_pg
    in_specs += [pl.BlockSpec((1, heads, d), lambda b, j, pt: (b, 0, 0))] * 2
    args += [k_new, v_new]
    if mode == "diff":
        for p in diff_params:
            in_specs.append(pl.BlockSpec((1, p.shape[1]), lambda b, j, pt: (0, 0)))
        args += list(diff_params)
    grid_spec = pltpu.PrefetchScalarGridSpec(
        num_scalar_prefetch=1,
        grid=(bsz, steps),
        in_specs=in_specs,
        out_specs=pl.BlockSpec((1, heads, d), lambda b, j, pt: (b, 0, 0)),
        scratch_shapes=[pltpu.VMEM((nrow, 1), F32), pltpu.VMEM((nrow, 1), F32), pltpu.VMEM((nrow, d), F32)],
    )
    return pl.pallas_call(
        functools.partial(_decode_kernel, mode=mode, n_pg=n_pg, heads=heads, scale=scale, lam_init=lam_init),
        grid_spec=grid_spec,
        out_shape=jax.ShapeDtypeStruct((bsz, heads, d), BF16),
        compiler_params=_params("parallel", "arbitrary"),
        name="decode_" + mode,
    )(page_table, *args)


def _pad_rows(x, rows):
    return jnp.pad(x, ((0, rows - x.shape[0]), (0, 0)))


def _heads5(x2d, batch, t, heads):
    return x2d.reshape(1, batch, t, heads, HEAD_DIM)


def kernel(x_prompt, x_sample, cache_a_k, cache_a_v, cache_b_k, cache_b_v, cache_c_k, cache_c_v, cache_c_logf, page_table, p_prompt, p_sample, ffn1_norm, ffn1_w_gate, ffn1_w_up, ffn1_w_down, mix_norm, ab_w_in, b_lambda_q1, b_lambda_k1, b_lambda_q2, b_lambda_k2, b_subln, ab_w_out, c_w_in, c_b_f, c_w_out, ffn2_norm, ffn2_w_gate, ffn2_w_up, ffn2_w_down, ple_norm, ple_w_gate, ple_w_proj, final_norm):
    batch, seq, d_model = x_prompt.shape
    dec_batch, dec_seq, _ = x_sample.shape
    assert dec_seq == 1 and dec_batch <= SAMPLE_ROWS
    depth = ffn1_norm.shape[0]
    page = cache_a_k.shape[2]
    past_len = page_table.shape[1] * page
    a_heads, b_heads, c_heads = cache_a_k.shape[3], cache_b_k.shape[3], cache_c_k.shape[3]
    a_w, b_w, c_w = a_heads * HEAD_DIM, b_heads * HEAD_DIM, c_heads * HEAD_DIM
    m_p = batch * seq
    ple_dim = p_prompt.shape[-1]

    pos_p = jnp.arange(seq, dtype=jnp.int32)
    pos_s = jnp.full((SAMPLE_ROWS,), past_len, jnp.int32)
    rope_p = {k: _rope_tables(pos_p, k) for k in "ab"}
    rope_s = {k: _rope_tables(pos_s, k) for k in "ab"}

    copies = {}

    def weight(name, param, layer, col0=0):
        return copies.get((name, layer, col0), _W(param, layer))

    def ffn(x, norm, named_params, layer):
        out, made = _ffn(x, norm, *[weight(n, p, layer) for n, p in named_params])
        if made is not None:
            for (n, _), c in zip(named_params, made):
                copies[(n, layer, 0)] = _W(c)
        return out

    def proj(h, name, param, layer, col0, width, dtypes, rope_kind=None, tables=None):
        res, made = _proj(h, weight(name, param, layer, col0), col0, width, dtypes, rope_kind, tables)
        if made is not None:
            copies[(name, layer, col0)] = _W(made)
        return res

    def out_proj(lhs, name, param, layer, x):
        res, made = _out_proj(lhs, weight(name, param, layer), x)
        if made is not None:
            copies[(name, layer, 0)] = _W(made)
        return res

    xp = x_prompt.reshape(m_p, d_model)
    xs = _pad_rows(x_sample.reshape(dec_batch, d_model), SAMPLE_ROWS)
    final = {}
    outs = {k: [] for k in ("akp", "avp", "aks", "avs", "bkp", "bvp", "bks", "bvs",
                            "ckp", "cvp", "cfp", "cks", "cvs", "cfs")}

    def sample_heads(x2d, heads):
        return x2d[:dec_batch].reshape(dec_batch, heads, HEAD_DIM)

    for is_prompt, x, p_in in ((False, xs, p_sample), (True, xp, p_prompt)):
        for i in range(depth):
            x = ffn(x, ffn1_norm[i], [("ffn1_w_gate", ffn1_w_gate), ("ffn1_w_up", ffn1_w_up),
                                      ("ffn1_w_down", ffn1_w_down)], i)
            h = _rmsnorm(x, mix_norm[i], BF16)
            rope = rope_p if is_prompt else rope_s
            if i % 2 == 0:
                e = i // 2
                lam_init = 0.8 - 0.6 * math.exp(-0.3 * i)
                (qa,) = proj(h, "ab_w_in", ab_w_in, e, 0, a_w, [BF16], "a", rope["a"])
                (ka,) = proj(h, "ab_w_in", ab_w_in, e, a_w, a_w, [F32], "a", rope["a"])
                (va,) = proj(h, "ab_w_in", ab_w_in, e, 2 * a_w, a_w, [F32])
                (qb,) = proj(h, "ab_w_in", ab_w_in, e, 3 * a_w, b_w, [BF16], "b", rope["b"])
                (kb,) = proj(h, "ab_w_in", ab_w_in, e, 3 * a_w + b_w, b_w, [F32], "b", rope["b"])
                (vb,) = proj(h, "ab_w_in", ab_w_in, e, 3 * a_w + 2 * b_w, b_w, [F32])
                diff_params = [v[e].reshape(1, -1) for v in (b_lambda_q1, b_lambda_k1, b_lambda_q2, b_lambda_k2,
                                                             b_subln)]
                if is_prompt:
                    hs = _head_spec(seq)
                    a_out = _prompt_attn(_moba_prompt_kernel, "moba_prompt", batch, seq, a_heads,
                                         [qa, ka, va], [hs] * 3)
                    b_out = _prompt_attn(functools.partial(_diff_prompt_kernel, lam_init=lam_init), "diff_prompt",
                                         batch, seq, b_heads, [qb, kb, vb] + diff_params,
                                         [hs] * 3 + [_vec_spec(p.shape[1]) for p in diff_params])
                    for key, val, nh in (("akp", ka, a_heads), ("avp", va, a_heads),
                                         ("bkp", kb, b_heads), ("bvp", vb, b_heads)):
                        outs[key].append(val.reshape(batch, seq, nh, HEAD_DIM))
                else:
                    ka_h, va_h = sample_heads(ka, a_heads), sample_heads(va, a_heads)
                    kb_h, vb_h = sample_heads(kb, b_heads), sample_heads(vb, b_heads)
                    (qa_f32,) = proj(h, "ab_w_in", ab_w_in, e, 0, a_w, [F32], "a", rope["a"])
                    kmean = _kmean_pages(cache_a_k, e, page_table)
                    picks = _moba_select(kmean, sample_heads(qa_f32, a_heads))
                    a_o = _moba_decode(sample_heads(qa, a_heads), cache_a_k, cache_a_v, e, page_table, picks,
                                       ka_h, va_h)
                    qb_h = sample_heads(qb, b_heads)
                    lane = jnp.arange(HEAD_DIM) < HEAD_DIM // 2
                    qb2 = jnp.concatenate([jnp.where(lane, qb_h, 0), jnp.where(lane, 0, qb_h)], axis=1)
                    b_o = _decode_attn("diff", qb2, cache_b_k, cache_b_v, e, page_table, kb_h, vb_h,
                                       n_pg=8, scale=(HEAD_DIM // 2) ** -0.5, diff_params=diff_params,
                                       lam_init=lam_init)
                    a_out = _pad_rows(a_o.reshape(dec_batch, a_w), SAMPLE_ROWS)
                    b_out = _pad_rows(b_o.reshape(dec_batch, b_w), SAMPLE_ROWS)
                    for key, val in (("aks", ka_h), ("avs", va_h), ("bks", kb_h), ("bvs", vb_h)):
                        outs[key].append(val[:, None])
                x = out_proj([a_out, b_out], "ab_w_out", ab_w_out, e, x)
            else:
                o = i // 2
                (q,) = proj(h, "c_w_in", c_w_in, o, 0, c_w, [BF16])
                (k,) = proj(h, "c_w_in", c_w_in, o, c_w, c_w, [F32])
                (v,) = proj(h, "c_w_in", c_w_in, o, 2 * c_w, c_w, [F32])
                w_f = jnp.pad(c_w_in[o][:, 3 * c_w:], ((0, 0), (0, LANES - c_heads)))
                b_f = jnp.pad(c_b_f[o], (0, LANES - c_heads)).reshape(1, LANES)
                logf = _logf_proj(h, w_f, b_f)[:, :c_heads]
                if is_prompt:
                    c = _cumsum_time(logf.reshape(batch, seq, c_heads)).transpose(0, 2, 1)
                    hs = _head_spec(seq)
                    c_out = _prompt_attn(
                        _fox_prompt_kernel, "fox_prompt", batch, seq, c_heads,
                        [q, k, v, c.reshape(batch, c_heads, seq, 1)],
                        [hs] * 3 + [pl.BlockSpec((1, 1, seq, 1), lambda b, hh: (b, hh, 0, 0))])
                    outs["ckp"].append(k.reshape(batch, seq, c_heads, HEAD_DIM))
                    outs["cvp"].append(v.reshape(batch, seq, c_heads, HEAD_DIM))
                    outs["cfp"].append(logf.reshape(batch, seq, c_heads))
                else:
                    k_h, v_h = sample_heads(k, c_heads), sample_heads(v, c_heads)
                    lf_new = logf[:dec_batch]
                    bias = _suffix_bias(cache_c_logf, o, page_table, lf_new)
                    bias = bias.reshape(dec_batch, page_table.shape[1], 1, page * c_heads)
                    c_o = _decode_attn("fox", sample_heads(q, c_heads), cache_c_k, cache_c_v, o, page_table,
                                       k_h, v_h, n_pg=4, scale=HEAD_DIM ** -0.5, bias=bias)
                    c_out = _pad_rows(c_o.reshape(dec_batch, c_w), SAMPLE_ROWS)
                    outs["cks"].append(k_h[:, None])
                    outs["cvs"].append(v_h[:, None])
                    outs["cfs"].append(lf_new[:, None])
                x = out_proj([c_out], "c_w_out", c_w_out, o, x)
            x = ffn(x, ffn2_norm[i], [("ffn2_w_gate", ffn2_w_gate), ("ffn2_w_up", ffn2_w_up),
                                      ("ffn2_w_down", ffn2_w_down)], i)
            hp = _rmsnorm(x, ple_norm[i], BF16)
            p2d = p_in[i].reshape(-1, ple_dim)
            if not is_prompt:
                p2d = _pad_rows(p2d, SAMPLE_ROWS)
            x, made = _ple(hp, weight("ple_w_gate", ple_w_gate, i), p2d, ple_w_proj, i, x)
            if made is not None:
                copies[("ple_w_gate", i, 0)] = _W(made)
        final[is_prompt] = x

    y_prompt = _rmsnorm(final[True], final_norm, F32).reshape(batch, seq, d_model)
    y_sample = _rmsnorm(final[False], final_norm, F32)[:dec_batch].reshape(dec_batch, dec_seq, d_model)
    st = {k: jnp.stack(v) for k, v in outs.items()}
    return (y_prompt, y_sample,
            st["akp"], st["avp"], st["aks"], st["avs"],
            st["bkp"], st["bvp"], st["bks"], st["bvs"],
            st["ckp"], st["cvp"], st["cfp"], st["cks"], st["cvs"], st["cfs"])
```
